```python
import jax, jax.numpy as jnp
from jax import lax
import numpy as np

D_MODEL = 1024
BATCH = 32
SEQ = 2048
DEPTH = 2

N_MIXERS = 2
EPS = 1e-6
Q_BLOCK = 128
NEG_INF = -1e30
MEM_LEN = 256
XATTN_HEADS = 4
XATTN_WIDTH = D_MODEL // 4
XATTN_HEAD_DIM = XATTN_WIDTH // XATTN_HEADS
TOKEN_MIX_WIDTH = D_MODEL - XATTN_WIDTH
QK_NOPE_DIM = 64
QK_ROPE_DIM = 32
QK_HEAD_DIM = QK_NOPE_DIM + QK_ROPE_DIM
V_HEAD_DIM = 64
MLA_HEADS = TOKEN_MIX_WIDTH // V_HEAD_DIM
Q_LORA_RANK = D_MODEL // 4
KV_LORA_RANK = D_MODEL // 8
MLA_IN = Q_LORA_RANK + KV_LORA_RANK + QK_ROPE_DIM
ROPE_THETA = 10000.0
POOL_WINDOWS = (2, 4, 8, 16)
POOL_GROUPS = len(POOL_WINDOWS)
POOL_GROUP_WIDTH = TOKEN_MIX_WIDTH // POOL_GROUPS
N_EXPERT_GROUPS = 4
EXPERTS_PER_GROUP = 8
N_EXPERTS = N_EXPERT_GROUPS * EXPERTS_PER_GROUP
TOP_K = 2
D_EXPERT = D_MODEL // 4

kernel_name = 'hybrid_mla_pool_memxattn_hmoe'


def rmsnorm(x, g):
    x32 = x.astype(jnp.float32)
    y = x32 * lax.rsqrt(jnp.mean(x32 * x32, axis=-1, keepdims=True) + EPS)
    return (y * g.astype(jnp.float32)).astype(x.dtype)


def apply_rope(x, cos, sin):
    x1, x2 = jnp.split(x, 2, axis=-1)
    return jnp.concatenate([x1 * cos - x2 * sin, x2 * cos + x1 * sin], axis=-1)


def causal_mla_attention(q_nope, q_rope, k_nope, k_rope, v):
    S = q_nope.shape[1]
    scale = QK_HEAD_DIM ** -0.5
    outs = []
    for blk in range(S // Q_BLOCK):
        lo, hi = blk * Q_BLOCK, (blk + 1) * Q_BLOCK
        s = (jnp.einsum('bqhd,bkhd->bhqk', q_nope[:, lo:hi], k_nope[:, :hi])
             + jnp.einsum('bqhr,bkr->bhqk', q_rope[:, lo:hi], k_rope[:, :hi]))
        s = s.astype(jnp.float32) * scale
        q_idx = lo + jnp.arange(Q_BLOCK)[:, None]
        k_idx = jnp.arange(hi)[None, :]
        s = jnp.where(k_idx <= q_idx, s, NEG_INF)
        p = jax.nn.softmax(s, axis=-1).astype(v.dtype)
        outs.append(jnp.einsum('bhqk,bkhd->bqhd', p, v[:, :hi]))
    return jnp.concatenate(outs, axis=1)


def mla_mixer(z, g_q_lat, w_uq, g_kv_lat, w_ukv, cos, sin):
    B, S, _ = z.shape
    q_lat = z[..., :Q_LORA_RANK]
    kv_lat = z[..., Q_LORA_RANK:Q_LORA_RANK + KV_LORA_RANK]
    k_r = z[..., Q_LORA_RANK + KV_LORA_RANK:]
    q = (rmsnorm(q_lat, g_q_lat) @ w_uq).reshape(B, S, MLA_HEADS, QK_HEAD_DIM)
    q_nope = q[..., :QK_NOPE_DIM]
    q_rope = apply_rope(q[..., QK_NOPE_DIM:], cos[:, :, None, :], sin[:, :, None, :])
    kv = (rmsnorm(kv_lat, g_kv_lat) @ w_ukv).reshape(B, S, MLA_HEADS, QK_NOPE_DIM + V_HEAD_DIM)
    k_nope = kv[..., :QK_NOPE_DIM]
    v = kv[..., QK_NOPE_DIM:]
    k_rope = apply_rope(k_r, cos, sin)
    o = causal_mla_attention(q_nope, q_rope, k_nope, k_rope, v)
    return o.reshape(B, S, MLA_HEADS * V_HEAD_DIM)


def multiscale_pool(u, w_group, scale):
    B, S, C = u.shape
    ug = u.reshape(B, S, POOL_GROUPS, POOL_GROUP_WIDTH).astype(jnp.float32)
    csum = jnp.cumsum(ug, axis=1)
    steps = jnp.arange(1, S + 1)
    outs = []
    for g, win in enumerate(POOL_WINDOWS):
        cg = csum[:, :, g]
        shifted = jnp.pad(cg, ((0, 0), (win, 0), (0, 0)))[:, :S]
        count = jnp.minimum(steps, win).astype(jnp.float32)[None, :, None]
        outs.append((cg - shifted) / count - ug[:, :, g])
    d = jnp.stack(outs, axis=2).astype(u.dtype)
    y = jnp.einsum('bsgc,gcd->bsgd', d, w_group).reshape(B, S, C)
    return y * scale


def memory_cross_attention(qc, mem_k, mem_v):
    B, S, _ = qc.shape
    q = qc.reshape(B, S, XATTN_HEADS, XATTN_HEAD_DIM)
    s = jnp.einsum('bshd,bmhd->bhsm', q, mem_k).astype(jnp.float32) * (XATTN_HEAD_DIM ** -0.5)
    p = jax.nn.softmax(s, axis=-1).astype(qc.dtype)
    return jnp.einsum('bhsm,bmhd->bshd', p, mem_v).reshape(B, S, XATTN_WIDTH)


def hierarchical_moe(h, w_rg, b_rg, w_re, b_re, w_gate, w_up, w_down):
    B, S, D = h.shape
    t = h.reshape(-1, D)
    t32 = t.astype(jnp.float32)
    lg = t32 @ w_rg.astype(jnp.float32) + b_rg.astype(jnp.float32)
    pg = jax.nn.softmax(lg, axis=-1)
    g_sel = jnp.argmax(lg, axis=-1)
    gate_g = jnp.take_along_axis(pg, g_sel[:, None], axis=1)
    le = (t32 @ w_re.astype(jnp.float32) + b_re.astype(jnp.float32)).reshape(-1, N_EXPERT_GROUPS, EXPERTS_PER_GROUP)
    le_sel = jnp.take_along_axis(le, g_sel[:, None, None], axis=1)[:, 0]
    pe = jax.nn.softmax(le_sel, axis=-1)
    top_p, top_i = lax.top_k(pe, TOP_K)
    wts = gate_g * top_p / jnp.sum(top_p, axis=-1, keepdims=True)
    eid = g_sel[:, None] * EXPERTS_PER_GROUP + top_i
    comb = jnp.sum(jax.nn.one_hot(eid, N_EXPERTS, dtype=jnp.float32) * wts[..., None], axis=1).astype(t.dtype)
    y = jnp.zeros_like(t)
    for e in range(N_EXPERTS):
        a = jax.nn.silu(t @ w_gate[e]) * (t @ w_up[e])
        y = y + comb[:, e:e + 1] * (a @ w_down[e])
    return y.reshape(B, S, D)


def setup_inputs(seed: int = 0) -> dict:
    key = jax.random.key(seed)
    ks = iter(jax.random.split(key, 32))
    f32 = jnp.float32
    n_mla = (DEPTH + 1) // 2
    n_pool = DEPTH // 2

    def w(shape, fan_in):
        return jax.random.normal(next(ks), shape, f32) * (fan_in ** -0.5)

    def gain(shape, s=0.05):
        return 1.0 + s * jax.random.normal(next(ks), shape, f32)

    x = jax.random.normal(next(ks), (BATCH, SEQ, D_MODEL), f32)
    mem = jax.random.normal(next(ks), (BATCH, MEM_LEN, D_MODEL), f32)
    offset = jax.random.randint(next(ks), (BATCH, 1), 0, 4096, dtype=jnp.int32)
    positions = (offset + jnp.arange(SEQ, dtype=jnp.int32)[None, :]).astype(jnp.int32)
    return {
        'x': x,
        'mem': mem,
        'positions': positions,
        'g_mix': gain((DEPTH, D_MODEL)),
        'w_in_mla': w((n_mla, D_MODEL, MLA_IN + XATTN_WIDTH), D_MODEL),
        'g_q_lat': gain((n_mla, Q_LORA_RANK)),
        'w_uq': w((n_mla, Q_LORA_RANK, MLA_HEADS * QK_HEAD_DIM), Q_LORA_RANK),
        'g_kv_lat': gain((n_mla, KV_LORA_RANK)),
        'w_ukv': w((n_mla, KV_LORA_RANK, MLA_HEADS * (QK_NOPE_DIM + V_HEAD_DIM)), KV_LORA_RANK),
        'w_in_pool': w((n_pool, D_MODEL, TOKEN_MIX_WIDTH + XATTN_WIDTH), D_MODEL),
        'w_pool_mix': w((n_pool, POOL_GROUPS, POOL_GROUP_WIDTH, POOL_GROUP_WIDTH), POOL_GROUP_WIDTH),
        'pool_scale': gain((n_pool, TOKEN_MIX_WIDTH), 0.1),
        'g_mem': gain((D_MODEL,)),
        'w_mem_kv': w((D_MODEL, 2 * XATTN_WIDTH), D_MODEL),
        'w_o': w((DEPTH, D_MODEL, D_MODEL), D_MODEL),
        'g_ffn': gain((DEPTH, D_MODEL)),
        'w_router_group': w((DEPTH, D_MODEL, N_EXPERT_GROUPS), D_MODEL),
        'b_router_group': 0.01 * jax.random.normal(next(ks), (DEPTH, N_EXPERT_GROUPS), f32),
        'w_router_expert': w((DEPTH, D_MODEL, N_EXPERTS), D_MODEL),
        'b_router_expert': 0.01 * jax.random.normal(next(ks), (DEPTH, N_EXPERTS), f32),
        'w_expert_gate': w((DEPTH, N_EXPERTS, D_MODEL, D_EXPERT), D_MODEL),
        'w_expert_up': w((DEPTH, N_EXPERTS, D_MODEL, D_EXPERT), D_MODEL),
        'w_expert_down': w((DEPTH, N_EXPERTS, D_EXPERT, D_MODEL), D_EXPERT),
        'g_final': gain((D_MODEL,)),
    }


def reference(x, mem, positions, g_mix, w_in_mla, g_q_lat, w_uq, g_kv_lat, w_ukv, w_in_pool, w_pool_mix, pool_scale, g_mem, w_mem_kv, w_o, g_ffn, w_router_group, b_router_group, w_router_expert, b_router_expert, w_expert_gate, w_expert_up, w_expert_down, g_final):
    B = x.shape[0]
    M = mem.shape[1]
    mem_kv = rmsnorm(mem, g_mem) @ w_mem_kv
    mem_k = mem_kv[..., :XATTN_WIDTH].reshape(B, M, XATTN_HEADS, XATTN_HEAD_DIM)
    mem_v = mem_kv[..., XATTN_WIDTH:].reshape(B, M, XATTN_HEADS, XATTN_HEAD_DIM)
    inv_freq = ROPE_THETA ** (-jnp.arange(0, QK_ROPE_DIM, 2, dtype=jnp.float32) / QK_ROPE_DIM)
    ang = positions.astype(jnp.float32)[..., None] * inv_freq
    cos = jnp.cos(ang).astype(x.dtype)
    sin = jnp.sin(ang).astype(x.dtype)
    for i in range(DEPTH):
        h = rmsnorm(x, g_mix[i])
        j = i // N_MIXERS
        if i % N_MIXERS == 0:
            z = h @ w_in_mla[j]
            mix = mla_mixer(z[..., :MLA_IN], g_q_lat[j], w_uq[j], g_kv_lat[j], w_ukv[j], cos, sin)
            qc = z[..., MLA_IN:]
        else:
            z = h @ w_in_pool[j]
            mix = multiscale_pool(z[..., :TOKEN_MIX_WIDTH], w_pool_mix[j], pool_scale[j])
            qc = z[..., TOKEN_MIX_WIDTH:]
        xa = memory_cross_attention(qc, mem_k, mem_v)
        x = x + jnp.concatenate([mix, xa], axis=-1) @ w_o[i]
        x = x + hierarchical_moe(rmsnorm(x, g_ffn[i]), w_router_group[i], b_router_group[i], w_router_expert[i], b_router_expert[i], w_expert_gate[i], w_expert_up[i], w_expert_down[i])
    return rmsnorm(x, g_final)
```

```python
import functools

import jax
import jax.numpy as jnp
import numpy as np
from jax import lax
from jax.experimental import pallas as pl
from jax.experimental.pallas import tpu as pltpu

D_MODEL = 1024
EPS = 1e-6
NEG_INF = -1e30
MEM_LEN = 256
XATTN_HEADS = 4
XATTN_WIDTH = 256
XATTN_HEAD_DIM = 64
TOKEN_MIX_WIDTH = 768
QK_NOPE_DIM = 64
QK_ROPE_DIM = 32
QK_HEAD_DIM = 96
V_HEAD_DIM = 64
MLA_HEADS = 12
Q_LORA_RANK = 256
KV_LORA_RANK = 128
ROPE_THETA = 10000.0
POOL_WINDOWS = (2, 4, 8, 16)
POOL_GROUP_WIDTH = 192
N_EXPERT_GROUPS = 4
EXPERTS_PER_GROUP = 8
N_EXPERTS = 32
D_EXPERT = 256

LANES = 128
VMEM_LIMIT_BYTES = 56 * 1024 * 1024

TOKEN_TILE = 512
EXPERT_TILE = 256
ATTN_Q_TILE = 512
ATTN_K_TILE = 256
HEAD_BLOCK = LANES

PAIRS_PER_GROUP = EXPERTS_PER_GROUP * (EXPERTS_PER_GROUP - 1) // 2
N_CLASSES = N_EXPERT_GROUPS * PAIRS_PER_GROUP
CLASS_PAD = LANES
HALF = D_MODEL // 2
PACKED_ROW = HALF + LANES


def _class_tables():
    ea = np.zeros((CLASS_PAD,), np.int32)
    eb = np.zeros((CLASS_PAD,), np.int32)
    for g in range(N_EXPERT_GROUPS):
        for a in range(EXPERTS_PER_GROUP):
            for b in range(a + 1, EXPERTS_PER_GROUP):
                c = g * PAIRS_PER_GROUP + (a * (15 - a)) // 2 + (b - a - 1)
                ea[c] = g * EXPERTS_PER_GROUP + a
                eb[c] = g * EXPERTS_PER_GROUP + b
    return ea, eb


_CLASS_EA, _CLASS_EB = _class_tables()


def _cparams(sem):
    return pltpu.CompilerParams(dimension_semantics=sem, vmem_limit_bytes=VMEM_LIMIT_BYTES)


def _rms(x, g):
    return x * lax.rsqrt(jnp.mean(x * x, axis=-1, keepdims=True) + EPS) * g


def _dot(a, b):
    return jnp.dot(a, b, preferred_element_type=jnp.float32)


def _dot_nt(a, b):
    return lax.dot_general(a, b, (((1,), (1,)), ((), ())), preferred_element_type=jnp.float32)


def _bits(x):
    return lax.bitcast_convert_type(x, jnp.uint32)


def _pack_halves(y):
    return pltpu.pack_elementwise([y[:, :HALF], y[:, HALF:]], packed_dtype=jnp.bfloat16)


def _unpack_halves(w):
    lo = pltpu.unpack_elementwise(w, index=0, packed_dtype=jnp.bfloat16, unpacked_dtype=jnp.float32)
    hi = pltpu.unpack_elementwise(w, index=1, packed_dtype=jnp.bfloat16, unpacked_dtype=jnp.float32)
    return lo, hi


def _memkv_kernel(mem_ref, g_ref, w_ref, o_ref):
    h = _rms(mem_ref[0], g_ref[...]).astype(jnp.bfloat16)
    o_ref[0] = _dot(h, w_ref[...]).astype(jnp.bfloat16)


def _memkv(mem, g_mem, w_mem_kv):
    B, M, D = mem.shape
    return pl.pallas_call(
        _memkv_kernel,
        out_shape=jax.ShapeDtypeStruct((B, M, 2 * XATTN_WIDTH), jnp.bfloat16),
        grid=(B,),
        in_specs=[
            pl.BlockSpec((1, M, D), lambda b: (b, 0, 0)),
            pl.BlockSpec((1, D), lambda b: (0, 0)),
            pl.BlockSpec((D, 2 * XATTN_WIDTH), lambda b: (0, 0)),
        ],
        out_specs=pl.BlockSpec((1, M, 2 * XATTN_WIDTH), lambda b: (b, 0, 0)),
        compiler_params=_cparams(("arbitrary",)),
        name="memkv",
    )(mem, g_mem.reshape(1, D), w_mem_kv.astype(jnp.bfloat16))


def _in_proj_kernel(x_ref, g_ref, w_ref, z_ref):
    h = _rms(x_ref[...], g_ref[...]).astype(jnp.bfloat16)
    z_ref[...] = _dot(h, w_ref[...]).astype(jnp.bfloat16)


def _in_proj(x2d, g, w_bf16):
    T, D = x2d.shape
    N = w_bf16.shape[1]
    return pl.pallas_call(
        _in_proj_kernel,
        out_shape=jax.ShapeDtypeStruct((T, N), jnp.bfloat16),
        grid=(T // TOKEN_TILE,),
        in_specs=[
            pl.BlockSpec((TOKEN_TILE, D), lambda i: (i, 0)),
            pl.BlockSpec((1, D), lambda i: (0, 0)),
            pl.BlockSpec((D, N), lambda i: (0, 0)),
        ],
        out_specs=pl.BlockSpec((TOKEN_TILE, N), lambda i: (i, 0)),
        compiler_params=_cparams(("arbitrary",)),
        name="in_proj",
    )(x2d, g.reshape(1, D), w_bf16)


XATTN_ROWS = 256


def _xattn_rows(qc, mem_k, mem_v):
    lane = lax.broadcasted_iota(jnp.int32, (1, XATTN_WIDTH), 1)
    out = jnp.zeros((qc.shape[0], XATTN_WIDTH), jnp.float32)
    zero = jnp.zeros((), jnp.bfloat16)
    for h in range(XATTN_HEADS):
        hm = (lane >= h * XATTN_HEAD_DIM) & (lane < (h + 1) * XATTN_HEAD_DIM)
        s = _dot_nt(jnp.where(hm, qc, zero), mem_k) * (XATTN_HEAD_DIM ** -0.5)
        m = jnp.max(s, axis=-1, keepdims=True)
        p = jnp.exp(s - m)
        l = jnp.sum(p, axis=-1, keepdims=True)
        o = _dot(p.astype(jnp.bfloat16), jnp.where(hm, mem_v, zero))
        out = out + o / l
    return out


Z_MLA = Q_LORA_RANK + KV_LORA_RANK + 2 * HEAD_BLOCK + XATTN_WIDTH
PROJ_ROWS = 256


def _mla_kernel(z_ref, pos_ref, invf_ref, gq_ref, gkv_ref, wqa_ref, wqb_ref, wk_ref, wv_ref, memkv_ref,
                cat_ref, q_s, k_s, v_s, o_s):
    h = pl.program_id(1)
    S = z_ref.shape[1]
    scale = QK_HEAD_DIM ** -0.5

    @pl.when(h == 0)
    def _project():
        def chunk(c, carry):
            rows = pl.ds(pl.multiple_of(c * PROJ_ROWS, PROJ_ROWS), PROJ_ROWS)
            z = z_ref[0, rows, :]
            ang = pos_ref[0, rows, :].astype(jnp.float32) * invf_ref[...]
            cos = jnp.cos(ang)
            sin = jnp.sin(ang)
            q_lat = _rms(z[:, :Q_LORA_RANK].astype(jnp.float32), gq_ref[...]).astype(jnp.bfloat16)
            kv_lat = _rms(z[:, Q_LORA_RANK:Q_LORA_RANK + KV_LORA_RANK].astype(jnp.float32),
                          gkv_ref[...]).astype(jnp.bfloat16)
            o0 = Q_LORA_RANK + KV_LORA_RANK
            k_rope = (z[:, o0:o0 + HEAD_BLOCK].astype(jnp.float32) * cos
                      + z[:, o0 + HEAD_BLOCK:o0 + 2 * HEAD_BLOCK].astype(jnp.float32) * sin)
            for hh in range(MLA_HEADS):
                blk = slice(hh * HEAD_BLOCK, (hh + 1) * HEAD_BLOCK)
                qa = _dot(q_lat, wqa_ref[:, blk])
                qb = _dot(q_lat, wqb_ref[:, blk])
                q_s[hh, rows, :] = ((qa * cos + qb * sin) * scale).astype(jnp.bfloat16)
                k_s[hh, rows, :] = (_dot(kv_lat, wk_ref[:, blk]) + k_rope).astype(jnp.bfloat16)
            for p2 in range(MLA_HEADS // 2):
                blk = slice(p2 * HEAD_BLOCK, (p2 + 1) * HEAD_BLOCK)
                v_s[p2, rows, :] = _dot(kv_lat, wv_ref[:, blk]).astype(jnp.bfloat16)
            qc = z[:, o0 + 2 * HEAD_BLOCK:]
            mem_k = memkv_ref[0, :, :XATTN_WIDTH]
            mem_v = memkv_ref[0, :, XATTN_WIDTH:]
            cat_ref[0, rows, TOKEN_MIX_WIDTH:] = _xattn_rows(qc, mem_k, mem_v).astype(jnp.bfloat16)
            return carry

        lax.fori_loop(0, S // PROJ_ROWS, chunk, 0)

    hp = h // 2
    lane = lax.broadcasted_iota(jnp.int32, (1, HEAD_BLOCK), 1)
    own = (lane >= (h % 2) * V_HEAD_DIM) & (lane < (h % 2 + 1) * V_HEAD_DIM)
    kv_per_q = ATTN_Q_TILE // ATTN_K_TILE

    def q_block(i, carry0):
        rows = pl.ds(pl.multiple_of(i * ATTN_Q_TILE, ATTN_Q_TILE), ATTN_Q_TILE)
        q = q_s[h, rows, :]

        def step(j, carry, masked):
            m, l, acc = carry
            ks = pl.multiple_of(j * ATTN_K_TILE, ATTN_K_TILE)
            k = k_s[h, pl.ds(ks, ATTN_K_TILE), :]
            v = v_s[hp, pl.ds(ks, ATTN_K_TILE), :]
            s = _dot_nt(q, k)
            if masked:
                q_idx = i * ATTN_Q_TILE + lax.broadcasted_iota(jnp.int32, s.shape, 0)
                k_idx = ks + lax.broadcasted_iota(jnp.int32, s.shape, 1)
                s = jnp.where(k_idx <= q_idx, s, NEG_INF)
            m_new = jnp.maximum(m, jnp.max(s, axis=-1, keepdims=True))
            alpha = jnp.exp(m - m_new)
            p = jnp.exp(s - m_new)
            l = alpha * l + jnp.sum(p, axis=-1, keepdims=True)
            acc = alpha * acc + _dot(p.astype(jnp.bfloat16), v)
            return m_new, l, acc

        carry = (jnp.full((ATTN_Q_TILE, 1), NEG_INF, jnp.float32),
                 jnp.zeros((ATTN_Q_TILE, 1), jnp.float32),
                 jnp.zeros((ATTN_Q_TILE, HEAD_BLOCK), jnp.float32))
        carry = lax.fori_loop(0, i * kv_per_q, functools.partial(step, masked=False), carry)
        for jj in range(kv_per_q):
            carry = step(i * kv_per_q + jj, carry, True)
        m, l, acc = carry
        o = (acc / l).astype(jnp.bfloat16)

        @pl.when(h % 2 == 0)
        def _first():
            o_s[hp, rows, :] = jnp.where(own, o, jnp.zeros_like(o))

        @pl.when(h % 2 == 1)
        def _second():
            o_s[hp, rows, :] = jnp.where(own, o, o_s[hp, rows, :])

        return carry0

    lax.fori_loop(0, S // ATTN_Q_TILE, q_block, 0)

    @pl.when(h == MLA_HEADS - 1)
    def _emit():
        for p2 in range(MLA_HEADS // 2):
            cat_ref[0, :, p2 * HEAD_BLOCK:(p2 + 1) * HEAD_BLOCK] = o_s[p2]


def _mla_layer(z, positions, memkv, g_q, g_kv, w_uq, w_ukv):
    B, S, _ = z.shape
    wq = w_uq.reshape(Q_LORA_RANK, MLA_HEADS, QK_HEAD_DIM)
    nope, x1, x2 = wq[..., :QK_NOPE_DIM], wq[..., QK_NOPE_DIM:QK_NOPE_DIM + 16], wq[..., QK_NOPE_DIM + 16:]
    zpad = jnp.zeros((Q_LORA_RANK, MLA_HEADS, HEAD_BLOCK - QK_HEAD_DIM), w_uq.dtype)
    wqa = jnp.concatenate([nope, x1, x2, zpad], -1).reshape(Q_LORA_RANK, MLA_HEADS * HEAD_BLOCK)
    wqb = jnp.concatenate([jnp.zeros_like(nope), -x2, x1, zpad], -1).reshape(Q_LORA_RANK, MLA_HEADS * HEAD_BLOCK)
    wkv = w_ukv.reshape(KV_LORA_RANK, MLA_HEADS, QK_NOPE_DIM + V_HEAD_DIM)
    wk = jnp.concatenate([wkv[..., :QK_NOPE_DIM], jnp.zeros_like(wkv[..., :QK_NOPE_DIM])], -1)
    wk = wk.reshape(KV_LORA_RANK, MLA_HEADS * HEAD_BLOCK)
    wv = wkv[..., QK_NOPE_DIM:].reshape(KV_LORA_RANK, MLA_HEADS * V_HEAD_DIM)
    inv_freq = ROPE_THETA ** (-np.arange(0, QK_ROPE_DIM, 2, dtype=np.float32) / QK_ROPE_DIM)
    invf = np.zeros((1, HEAD_BLOCK), np.float32)
    invf[0, QK_NOPE_DIM:QK_NOPE_DIM + 16] = inv_freq
    invf[0, QK_NOPE_DIM + 16:QK_NOPE_DIM + 32] = inv_freq
    bf = jnp.bfloat16
    const = lambda b, h: (0, 0)
    return pl.pallas_call(
        _mla_kernel,
        out_shape=jax.ShapeDtypeStruct((B, S, D_MODEL), jnp.bfloat16),
        grid=(B, MLA_HEADS),
        in_specs=[
            pl.BlockSpec((1, S, Z_MLA), lambda b, h: (b, 0, 0)),
            pl.BlockSpec((1, S, 1), lambda b, h: (b, 0, 0)),
            pl.BlockSpec((1, HEAD_BLOCK), const),
            pl.BlockSpec((1, Q_LORA_RANK), const),
            pl.BlockSpec((1, KV_LORA_RANK), const),
            pl.BlockSpec((Q_LORA_RANK, MLA_HEADS * HEAD_BLOCK), const),
            pl.BlockSpec((Q_LORA_RANK, MLA_HEADS * HEAD_BLOCK), const),
            pl.BlockSpec((KV_LORA_RANK, MLA_HEADS * HEAD_BLOCK), const),
            pl.BlockSpec((KV_LORA_RANK, MLA_HEADS * V_HEAD_DIM), const),
            pl.BlockSpec((1, MEM_LEN, 2 * XATTN_WIDTH), lambda b, h: (b, 0, 0)),
        ],
        out_specs=pl.BlockSpec((1, S, D_MODEL), lambda b, h: (b, 0, 0)),
        scratch_shapes=[
            pltpu.VMEM((MLA_HEADS, S, HEAD_BLOCK), bf),
            pltpu.VMEM((MLA_HEADS, S, HEAD_BLOCK), bf),
            pltpu.VMEM((MLA_HEADS // 2, S, HEAD_BLOCK), bf),
            pltpu.VMEM((MLA_HEADS // 2, S, HEAD_BLOCK), bf),
        ],
        compiler_params=_cparams(("arbitrary", "arbitrary")),
        name="mla_mixer",
    )(z, positions.reshape(B, S, 1), jnp.asarray(invf), g_q.reshape(1, -1), g_kv.reshape(1, -1),
      wqa.astype(bf), wqb.astype(bf), wk.astype(bf), wv.astype(bf), memkv)


def _mla_in_weight(w_in):
    o0 = Q_LORA_RANK + KV_LORA_RANK
    x1, x2 = w_in[:, o0:o0 + 16], w_in[:, o0 + 16:o0 + 32]
    z64 = jnp.zeros((D_MODEL, QK_NOPE_DIM), w_in.dtype)
    z32 = jnp.zeros((D_MODEL, HEAD_BLOCK - QK_HEAD_DIM), w_in.dtype)
    return jnp.concatenate([w_in[:, :o0], z64, x1, x2, z32, z64, -x2, x1, z32, w_in[:, o0 + 32:]], axis=1)


def _shift_rows(x, k):
    rolled = pltpu.roll(x, k, 0)
    row = lax.broadcasted_iota(jnp.int32, x.shape, 0)
    return jnp.where(row >= k, rolled, 0.0)


def _pool_kernel(z_ref, wmix_ref, scale_ref, memkv_ref, cat_ref, d_s):
    S = z_ref.shape[1]
    step = (lax.broadcasted_iota(jnp.int32, (S, 1), 0) + 1).astype(jnp.float32)
    for ct in range(TOKEN_MIX_WIDTH // LANES):
        cols = slice(ct * LANES, (ct + 1) * LANES)
        u = z_ref[0, :, cols].astype(jnp.float32)
        groups = sorted({(ct * LANES) // POOL_GROUP_WIDTH, (ct * LANES + LANES - 1) // POOL_GROUP_WIDTH})
        sums, cur, w = {}, u, 1
        while w < POOL_WINDOWS[groups[-1]]:
            cur = cur + _shift_rows(cur, w)
            w *= 2
            sums[w] = cur
        mean = sums[POOL_WINDOWS[groups[0]]] / jnp.minimum(step, float(POOL_WINDOWS[groups[0]]))
        if len(groups) == 2:
            win = POOL_WINDOWS[groups[1]]
            col = ct * LANES + lax.broadcasted_iota(jnp.int32, (1, LANES), 1)
            mean = jnp.where(col < groups[1] * POOL_GROUP_WIDTH, mean, sums[win] / jnp.minimum(step, float(win)))
        d_s[:, cols] = (mean - u).astype(jnp.bfloat16)
    y = _dot(d_s[...], wmix_ref[...]) * scale_ref[...]
    cat_ref[0, :, :TOKEN_MIX_WIDTH] = y.astype(jnp.bfloat16)
    mem_k = memkv_ref[0, :, :XATTN_WIDTH]
    mem_v = memkv_ref[0, :, XATTN_WIDTH:]
    for xc in range(S // XATTN_ROWS):
        rows = pl.ds(xc * XATTN_ROWS, XATTN_ROWS)
        qc = z_ref[0, rows, TOKEN_MIX_WIDTH:]
        cat_ref[0, rows, TOKEN_MIX_WIDTH:] = _xattn_rows(qc, mem_k, mem_v).astype(jnp.bfloat16)


def _pool_layer(z, memkv, w_pool_mix, pool_scale):
    B, S, _ = z.shape
    wmix = jnp.zeros((TOKEN_MIX_WIDTH, TOKEN_MIX_WIDTH), w_pool_mix.dtype)
    for g in range(len(POOL_WINDOWS)):
        blk = slice(g * POOL_GROUP_WIDTH, (g + 1) * POOL_GROUP_WIDTH)
        wmix = wmix.at[blk, blk].set(w_pool_mix[g])
    return pl.pallas_call(
        _pool_kernel,
        out_shape=jax.ShapeDtypeStruct((B, S, D_MODEL), jnp.bfloat16),
        grid=(B,),
        in_specs=[
            pl.BlockSpec((1, S, D_MODEL), lambda b: (b, 0, 0)),
            pl.BlockSpec((TOKEN_MIX_WIDTH, TOKEN_MIX_WIDTH), lambda b: (0, 0)),
            pl.BlockSpec((1, TOKEN_MIX_WIDTH), lambda b: (0, 0)),
            pl.BlockSpec((1, MEM_LEN, 2 * XATTN_WIDTH), lambda b: (b, 0, 0)),
        ],
        out_specs=pl.BlockSpec((1, S, D_MODEL), lambda b: (b, 0, 0)),
        scratch_shapes=[pltpu.VMEM((S, TOKEN_MIX_WIDTH), jnp.bfloat16)],
        compiler_params=_cparams(("arbitrary",)),
        name="pool_mixer",
    )(z, wmix.astype(jnp.bfloat16), pool_scale.reshape(1, -1), memkv)


def _post_kernel(cat_ref, x_ref, wo_ref, g_ref, wr_hi_ref, wr_lo_ref, br_ref,
                 x1_ref, hp_ref, cls_ref, cnt_ref):
    i = pl.program_id(0)
    tm = x_ref.shape[0]
    x1 = x_ref[...] + _dot(cat_ref[...], wo_ref[...])
    x1_ref[...] = x1
    hn = _rms(x1, g_ref[...])
    hi = hn.astype(jnp.bfloat16)
    lo = (hn - hi.astype(jnp.float32)).astype(jnp.bfloat16)
    logits = (_dot(hi, wr_hi_ref[...]) + _dot(lo, wr_hi_ref[...]) + _dot(hi, wr_lo_ref[...])
              + br_ref[...])
    lane = lax.broadcasted_iota(jnp.int32, (tm, LANES), 1)
    lane_f = lane.astype(jnp.float32)

    def first_argmax(v):
        m = jnp.max(v, axis=-1, keepdims=True)
        idx = jnp.min(jnp.where(v == m, lane_f, float(LANES)), axis=-1, keepdims=True)
        return m, idx.astype(jnp.int32)

    is_g = lane < N_EXPERT_GROUPS
    gmax, g_sel = first_argmax(jnp.where(is_g, logits, NEG_INF))
    gate_g = 1.0 / jnp.sum(jnp.where(is_g, jnp.exp(logits - gmax), 0.0), axis=-1, keepdims=True)
    lo_lane = N_EXPERT_GROUPS + EXPERTS_PER_GROUP * g_sel
    le = jnp.where((lane >= lo_lane) & (lane < lo_lane + EXPERTS_PER_GROUP), logits, NEG_INF)
    m1, i1 = first_argmax(le)
    m2, i2 = first_argmax(jnp.where(lane == i1, NEG_INF, le))
    e2 = jnp.exp(m2 - m1)
    w1 = gate_g / (1.0 + e2)
    w2 = gate_g * e2 / (1.0 + e2)
    j1, j2 = i1 - lo_lane, i2 - lo_lane
    first_low = j1 < j2
    a = jnp.where(first_low, j1, j2)
    b = jnp.where(first_low, j2, j1)
    w_a = jnp.where(first_low, w1, w2)
    w_b = jnp.where(first_low, w2, w1)
    cls = g_sel * PAIRS_PER_GROUP + ((a * (15 - a)) >> 1) + (b - a - 1)

    hp_ref[:, :HALF] = _pack_halves(hn)
    wa_bits = _bits(jnp.broadcast_to(w_a, (tm, LANES)))
    wb_bits = _bits(jnp.broadcast_to(w_b, (tm, LANES)))
    hp_ref[:, HALF:] = jnp.where(lane == 0, wa_bits, jnp.where(lane == 1, wb_bits, jnp.uint32(0)))

    cls_l0 = jnp.where(lane == 0, cls.astype(jnp.float32), 0.0).astype(jnp.bfloat16)
    sel = (lax.broadcasted_iota(jnp.int32, (8, LANES), 1) == 0).astype(jnp.bfloat16)
    cls_row = _dot_nt(sel, cls_l0)
    cls_ref[0] = cls_row.astype(jnp.int32)

    cid = lax.broadcasted_iota(jnp.int32, (CLASS_PAD, tm), 0).astype(jnp.float32)
    onehot_t = (cid == jnp.broadcast_to(cls_row[0:1, :], (CLASS_PAD, tm))).astype(jnp.float32)
    tile_cnt = jnp.broadcast_to(jnp.sum(onehot_t, axis=-1, keepdims=True), (CLASS_PAD, LANES))

    @pl.when(i == 0)
    def _init():
        cnt_ref[...] = tile_cnt

    @pl.when(i > 0)
    def _acc():
        cnt_ref[...] = cnt_ref[...] + tile_cnt


def _post(cat2d, x2d, w_o, g_ffn, w_rg, b_rg, w_re, b_re):
    T, D = x2d.shape
    n = T // TOKEN_TILE
    wr = jnp.zeros((D, LANES), jnp.float32)
    wr = wr.at[:, :N_EXPERT_GROUPS].set(w_rg).at[:, N_EXPERT_GROUPS:N_EXPERT_GROUPS + N_EXPERTS].set(w_re)
    wr_hi = wr.astype(jnp.bfloat16)
    wr_lo = (wr - wr_hi.astype(jnp.float32)).astype(jnp.bfloat16)
    br = jnp.zeros((1, LANES), jnp.float32)
    br = br.at[0, :N_EXPERT_GROUPS].set(b_rg).at[0, N_EXPERT_GROUPS:N_EXPERT_GROUPS + N_EXPERTS].set(b_re)
    const = lambda i: (0, 0)
    return pl.pallas_call(
        _post_kernel,
        out_shape=(
            jax.ShapeDtypeStruct((T, D), jnp.float32),
            jax.ShapeDtypeStruct((T, PACKED_ROW), jnp.uint32),
            jax.ShapeDtypeStruct((n, 8, TOKEN_TILE), jnp.int32),
            jax.ShapeDtypeStruct((CLASS_PAD, LANES), jnp.float32),
        ),
        grid=(n,),
        in_specs=[
            pl.BlockSpec((TOKEN_TILE, D), lambda i: (i, 0)),
            pl.BlockSpec((TOKEN_TILE, D), lambda i: (i, 0)),
            pl.BlockSpec((D, D), const),
            pl.BlockSpec((1, D), const),
            pl.BlockSpec((D, LANES), const),
            pl.BlockSpec((D, LANES), const),
            pl.BlockSpec((1, LANES), const),
        ],
        out_specs=(
            pl.BlockSpec((TOKEN_TILE, D), lambda i: (i, 0)),
            pl.BlockSpec((TOKEN_TILE, PACKED_ROW), lambda i: (i, 0)),
            pl.BlockSpec((1, 8, TOKEN_TILE), lambda i: (i, 0, 0)),
            pl.BlockSpec((CLASS_PAD, LANES), const),
        ),
        compiler_params=_cparams(("arbitrary",)),
        name="out_proj_router",
    )(cat2d, x2d, w_o.astype(jnp.bfloat16), g_ffn.reshape(1, D), wr_hi, wr_lo, br)


def _max_expert_tiles(T):
    return T // EXPERT_TILE + N_CLASSES


def _slots_kernel(cls_ref, cnt_ref, tri_ref, ltri_ref, pos_ref, tcls_ref, base_ref):
    i = pl.program_id(0)
    tm = cls_ref.shape[2]
    n_lane = tcls_ref.shape[1]

    @pl.when(i == 0)
    def _offsets():
        tiles = jnp.floor((cnt_ref[...] + (EXPERT_TILE - 1)) * (1.0 / EXPERT_TILE))
        before = _dot(ltri_ref[...], tiles.astype(jnp.bfloat16))
        base_ref[...] = before * EXPERT_TILE
        upto = (before + tiles)[:, 0:1]
        k = lax.broadcasted_iota(jnp.int32, (CLASS_PAD, n_lane), 1).astype(jnp.float32)
        cid = lax.broadcasted_iota(jnp.int32, (CLASS_PAD, n_lane), 0)
        done = jnp.where((upto <= k) & (cid < N_CLASSES), 1.0, 0.0)
        tile_cls = jnp.sum(done, axis=0, keepdims=True)
        total = jnp.max(jnp.where(cid < N_CLASSES, jnp.broadcast_to(upto, (CLASS_PAD, n_lane)), 0.0),
                        axis=0, keepdims=True)
        row = lax.broadcasted_iota(jnp.int32, (8, n_lane), 0)
        tcls_ref[...] = jnp.where(row == 0, jnp.broadcast_to(tile_cls, (8, n_lane)),
                                  jnp.broadcast_to(total, (8, n_lane))).astype(jnp.int32)

    cls_row = cls_ref[0, 0:1, :]
    cid = lax.broadcasted_iota(jnp.int32, (CLASS_PAD, tm), 0)
    onehot_t = cid == jnp.broadcast_to(cls_row, (CLASS_PAD, tm))
    oh = onehot_t.astype(jnp.bfloat16)
    earlier = _dot(oh, tri_ref[...])
    base = base_ref[:, 0:1]
    slot = jnp.sum(jnp.where(onehot_t, earlier + base, 0.0), axis=0, keepdims=True)
    pos_ref[0] = slot.astype(jnp.int32)
    base_ref[...] = base_ref[...] + jnp.sum(onehot_t.astype(jnp.float32), axis=-1, keepdims=True)


def _slots(cls, cnt):
    n, _, tm = cls.shape
    T = n * tm
    n_lane = pl.cdiv(_max_expert_tiles(T), LANES) * LANES
    tri = jnp.asarray(np.triu(np.ones((tm, tm), np.float32), 1), jnp.bfloat16)
    ltri = jnp.asarray(np.tril(np.ones((CLASS_PAD, CLASS_PAD), np.float32), -1), jnp.bfloat16)
    return pl.pallas_call(
        _slots_kernel,
        out_shape=(
            jax.ShapeDtypeStruct((n, 1, tm), jnp.int32),
            jax.ShapeDtypeStruct((8, n_lane), jnp.int32),
        ),
        grid=(n,),
        in_specs=[
            pl.BlockSpec((1, 8, tm), lambda i: (i, 0, 0)),
            pl.BlockSpec((CLASS_PAD, LANES), lambda i: (0, 0)),
            pl.BlockSpec((tm, tm), lambda i: (0, 0)),
            pl.BlockSpec((CLASS_PAD, CLASS_PAD), lambda i: (0, 0)),
        ],
        out_specs=(
            pl.BlockSpec((1, 1, tm), lambda i: (i, 0, 0)),
            pl.BlockSpec((8, n_lane), lambda i: (0, 0)),
        ),
        scratch_shapes=[pltpu.VMEM((CLASS_PAD, LANES), jnp.float32)],
        compiler_params=_cparams(("arbitrary",)),
        name="moe_slots",
    )(cls, cnt, tri, ltri)


def _row_copy(src_ref, src_row, dst_ref, dst_row, sem):
    return pltpu.make_async_copy(src_ref.at[pl.ds(src_row, 1), :], dst_ref.at[pl.ds(dst_row, 1), :], sem)


def _dispatch_kernel(pos_ref, hp_ref, xs_in_ref, xs_ref, sem):
    del xs_in_ref
    tm = hp_ref.shape[0]

    def start(r, c):
        _row_copy(hp_ref, r, xs_ref, pos_ref[0, 0, r], sem).start()
        return c

    lax.fori_loop(0, tm, start, 0, unroll=8)

    def wait(r, c):
        _row_copy(hp_ref, 0, xs_ref, 0, sem).wait()
        return c

    lax.fori_loop(0, tm, wait, 0, unroll=8)


def _dispatch(pos, hp, n_rows):
    T, W = hp.shape
    n = T // TOKEN_TILE
    return pl.pallas_call(
        _dispatch_kernel,
        out_shape=jax.ShapeDtypeStruct((n_rows, W), jnp.uint32),
        grid=(n,),
        in_specs=[
            pl.BlockSpec((1, 1, TOKEN_TILE), lambda i: (i, 0, 0), memory_space=pltpu.SMEM),
            pl.BlockSpec((TOKEN_TILE, W), lambda i: (i, 0)),
            pl.BlockSpec(memory_space=pl.ANY),
        ],
        out_specs=pl.BlockSpec(memory_space=pl.ANY),
        scratch_shapes=[pltpu.SemaphoreType.DMA],
        input_output_aliases={2: 0},
        compiler_params=_cparams(("arbitrary",)),
        name="moe_dispatch",
    )(pos, hp, jnp.zeros((n_rows, W), jnp.uint32))


def _expert_kernel(ea_ref, eb_ref, nt_ref, xs_ref, wgu_a_ref, wgu_b_ref, wd_a_ref, wd_b_ref, ys_ref):
    i = pl.program_id(0)

    @pl.when(i < nt_ref[0])
    def _run():
        lo, hi = _unpack_halves(xs_ref[:, :HALF])
        lo = lo.astype(jnp.bfloat16)
        hi = hi.astype(jnp.bfloat16)
        gates = lax.bitcast_convert_type(xs_ref[:, HALF:], jnp.float32)
        y = None
        for wgu_ref, wd_ref, lane in ((wgu_a_ref, wd_a_ref, 0), (wgu_b_ref, wd_b_ref, 1)):
            gu = _dot(lo, wgu_ref[:HALF, :]) + _dot(hi, wgu_ref[HALF:, :])
            g, u = gu[:, :D_EXPERT], gu[:, D_EXPERT:]
            act = (g * jax.nn.sigmoid(g)) * u * gates[:, lane:lane + 1]
            part = _dot(act.astype(jnp.bfloat16), wd_ref[...])
            y = part if y is None else y + part
        ys_ref[...] = _pack_halves(y)

    @pl.when(i >= nt_ref[0])
    def _unused():
        ys_ref[...] = jnp.zeros(ys_ref.shape, ys_ref.dtype)


def _experts(xs, tile_ea, tile_eb, n_tiles, w_gu, w_down):
    n_rows = xs.shape[0]
    max_tiles = n_rows // EXPERT_TILE

    def row_map(i, ea, eb, nt):
        return (jnp.minimum(i, nt[0] - 1), 0)

    return pl.pallas_call(
        _expert_kernel,
        out_shape=jax.ShapeDtypeStruct((n_rows, HALF), jnp.uint32),
        grid_spec=pltpu.PrefetchScalarGridSpec(
            num_scalar_prefetch=3,
            grid=(max_tiles,),
            in_specs=[
                pl.BlockSpec((EXPERT_TILE, PACKED_ROW), row_map),
                pl.BlockSpec((None, D_MODEL, 2 * D_EXPERT), lambda i, ea, eb, nt: (ea[i], 0, 0)),
                pl.BlockSpec((None, D_MODEL, 2 * D_EXPERT), lambda i, ea, eb, nt: (eb[i], 0, 0)),
                pl.BlockSpec((None, D_EXPERT, D_MODEL), lambda i, ea, eb, nt: (ea[i], 0, 0)),
                pl.BlockSpec((None, D_EXPERT, D_MODEL), lambda i, ea, eb, nt: (eb[i], 0, 0)),
            ],
            out_specs=pl.BlockSpec((EXPERT_TILE, HALF), lambda i, ea, eb, nt: (i, 0)),
        ),
        compiler_params=_cparams(("arbitrary",)),
        name="moe_experts",
    )(tile_ea, tile_eb, n_tiles, xs, w_gu, w_gu, w_down, w_down)


def _gather_rows(pos_ref, ys_ref, ybuf, sem):
    tm = ybuf.shape[0]

    def start(r, c):
        _row_copy(ys_ref, pos_ref[0, 0, r], ybuf, r, sem).start()
        return c

    lax.fori_loop(0, tm, start, 0, unroll=8)

    def wait(r, c):
        _row_copy(ys_ref, 0, ybuf, 0, sem).wait()
        return c

    lax.fori_loop(0, tm, wait, 0, unroll=8)


def _combine_proj_kernel(pos_ref, x1_ref, ys_ref, g_ref, w_ref, x2_ref, z_ref, ybuf, sem):
    _gather_rows(pos_ref, ys_ref, ybuf, sem)
    lo, hi = _unpack_halves(ybuf[...])
    x2 = x1_ref[...] + jnp.concatenate([lo, hi], axis=1)
    x2_ref[...] = x2
    h = _rms(x2, g_ref[...]).astype(jnp.bfloat16)
    z_ref[...] = _dot(h, w_ref[...]).astype(jnp.bfloat16)


def _combine_final_kernel(pos_ref, x1_ref, ys_ref, g_ref, out_ref, ybuf, sem):
    _gather_rows(pos_ref, ys_ref, ybuf, sem)
    lo, hi = _unpack_halves(ybuf[...])
    out_ref[...] = _rms(x1_ref[...] + jnp.concatenate([lo, hi], axis=1), g_ref[...])


def _combine(pos, x1, ys, g, w_next=None):
    T, D = x1.shape
    n = T // TOKEN_TILE
    in_specs = [
        pl.BlockSpec((1, 1, TOKEN_TILE), lambda i: (i, 0, 0), memory_space=pltpu.SMEM),
        pl.BlockSpec((TOKEN_TILE, D), lambda i: (i, 0)),
        pl.BlockSpec(memory_space=pl.ANY),
        pl.BlockSpec((1, D), lambda i: (0, 0)),
    ]
    scratch = [pltpu.VMEM((TOKEN_TILE, HALF), jnp.uint32), pltpu.SemaphoreType.DMA]
    row_spec = pl.BlockSpec((TOKEN_TILE, D), lambda i: (i, 0))
    if w_next is None:
        return pl.pallas_call(
            _combine_final_kernel,
            out_shape=jax.ShapeDtypeStruct((T, D), jnp.float32),
            grid=(n,), in_specs=in_specs, out_specs=row_spec, scratch_shapes=scratch,
            compiler_params=_cparams(("arbitrary",)), name="moe_combine_final",
        )(pos, x1, ys, g.reshape(1, D))
    N = w_next.shape[1]
    return pl.pallas_call(
        _combine_proj_kernel,
        out_shape=(jax.ShapeDtypeStruct((T, D), jnp.float32), jax.ShapeDtypeStruct((T, N), jnp.bfloat16)),
        grid=(n,),
        in_specs=in_specs + [pl.BlockSpec((D, N), lambda i: (0, 0))],
        out_specs=(row_spec, pl.BlockSpec((TOKEN_TILE, N), lambda i: (i, 0))),
        scratch_shapes=scratch,
        compiler_params=_cparams(("arbitrary",)), name="moe_combine_proj",
    )(pos, x1, ys, g.reshape(1, D), w_next)


def _moe_sorted(cat2d, x2d, w_o, g_ffn, w_rg, b_rg, w_re, b_re, w_gate, w_up, w_down):
    T = x2d.shape[0]
    x1, hp, cls, cnt = _post(cat2d, x2d, w_o, g_ffn, w_rg, b_rg, w_re, b_re)
    pos, tcls = _slots(cls, cnt)
    max_tiles = _max_expert_tiles(T)
    n_tiles = tcls[1, 0:1]
    tile_cls = jnp.minimum(tcls[0, :max_tiles], tcls[0, jnp.maximum(n_tiles[0] - 1, 0)])
    tile_ea = jnp.asarray(_CLASS_EA)[tile_cls]
    tile_eb = jnp.asarray(_CLASS_EB)[tile_cls]
    xs = _dispatch(pos, hp, max_tiles * EXPERT_TILE)
    w_gu = jnp.concatenate([w_gate, w_up], axis=-1).astype(jnp.bfloat16)
    ys = _experts(xs, tile_ea, tile_eb, n_tiles, w_gu, w_down.astype(jnp.bfloat16))
    return x1, pos, ys


def kernel(x, mem, positions, g_mix, w_in_mla, g_q_lat, w_uq, g_kv_lat, w_ukv, w_in_pool, w_pool_mix, pool_scale, g_mem, w_mem_kv, w_o, g_ffn, w_router_group, b_router_group, w_router_expert, b_router_expert, w_expert_gate, w_expert_up, w_expert_down, g_final):
    B, S, D = x.shape
    T = B * S
    depth = g_mix.shape[0]
    bf = jnp.bfloat16
    memkv = _memkv(mem, g_mem, w_mem_kv)
    x2d = x.reshape(T, D)
    z = _in_proj(x2d, g_mix[0], _mla_in_weight(w_in_mla[0]).astype(bf))
    for i in range(depth):
        j = i // 2
        if i % 2 == 0:
            cat = _mla_layer(z.reshape(B, S, -1), positions, memkv, g_q_lat[j], g_kv_lat[j], w_uq[j], w_ukv[j])
        else:
            cat = _pool_layer(z.reshape(B, S, -1), memkv, w_pool_mix[j], pool_scale[j])
        x1, pos, ys = _moe_sorted(
            cat.reshape(T, D), x2d, w_o[i], g_ffn[i], w_router_group[i], b_router_group[i],
            w_router_expert[i], b_router_expert[i], w_expert_gate[i], w_expert_up[i], w_expert_down[i])
        if i + 1 == depth:
            return _combine(pos, x1, ys, g_final).reshape(B, S, D)
        nxt = i + 1
        w_next = (_mla_in_weight(w_in_mla[nxt // 2]) if nxt % 2 == 0 else w_in_pool[nxt // 2]).astype(bf)
        x2d, z = _combine(pos, x1, ys, g_mix[nxt], w_next)
```

```python
import functools

import jax
import jax.numpy as jnp
import numpy as np
from jax import lax
from jax.experimental import pallas as pl
from jax.experimental.pallas import tpu as pltpu

D_MODEL = 1024
EPS = 1e-6
NEG_INF = -1e30
MEM_LEN = 256
XATTN_HEADS = 4
XATTN_WIDTH = 256
XATTN_HEAD_DIM = 64
TOKEN_MIX_WIDTH = 768
QK_NOPE_DIM = 64
QK_ROPE_DIM = 32
QK_HEAD_DIM = 96
V_HEAD_DIM = 64
MLA_HEADS = 12
Q_LORA_RANK = 256
KV_LORA_RANK = 128
ROPE_THETA = 10000.0
POOL_WINDOWS = (2, 4, 8, 16)
POOL_GROUP_WIDTH = 192
N_EXPERT_GROUPS = 4
EXPERTS_PER_GROUP = 8
N_EXPERTS = 32
D_EXPERT = 256

LANES = 128
VMEM_LIMIT_BYTES = 56 * 1024 * 1024

TOKEN_TILE = 512
EXPERT_TILE = 256
ATTN_TILE = 512
HEAD_BLOCK = LANES

PAIRS_PER_GROUP = EXPERTS_PER_GROUP * (EXPERTS_PER_GROUP - 1) // 2
N_CLASSES = N_EXPERT_GROUPS * PAIRS_PER_GROUP
CLASS_PAD = LANES
HALF = D_MODEL // 2
PACKED_ROW = HALF + LANES


def _class_tables():
    ea = np.zeros((CLASS_PAD,), np.int32)
    eb = np.zeros((CLASS_PAD,), np.int32)
    for g in range(N_EXPERT_GROUPS):
        for a in range(EXPERTS_PER_GROUP):
            for b in range(a + 1, EXPERTS_PER_GROUP):
                c = g * PAIRS_PER_GROUP + (a * (15 - a)) // 2 + (b - a - 1)
                ea[c] = g * EXPERTS_PER_GROUP + a
                eb[c] = g * EXPERTS_PER_GROUP + b
    return ea, eb


_CLASS_EA, _CLASS_EB = _class_tables()


def _cparams(sem):
    return pltpu.CompilerParams(dimension_semantics=sem, vmem_limit_bytes=VMEM_LIMIT_BYTES)


def _rms(x, g):
    return x * lax.rsqrt(jnp.mean(x * x, axis=-1, keepdims=True) + EPS) * g


def _dot(a, b):
    return jnp.dot(a, b, preferred_element_type=jnp.float32)


def _dot_nt(a, b):
    return lax.dot_general(a, b, (((1,), (1,)), ((), ())), preferred_element_type=jnp.float32)


def _bits(x):
    return lax.bitcast_convert_type(x, jnp.uint32)


def _pack_halves(y):
    return pltpu.pack_elementwise([y[:, :HALF], y[:, HALF:]], packed_dtype=jnp.bfloat16)


def _unpack_halves(w):
    lo = pltpu.unpack_elementwise(w, index=0, packed_dtype=jnp.bfloat16, unpacked_dtype=jnp.float32)
    hi = pltpu.unpack_elementwise(w, index=1, packed_dtype=jnp.bfloat16, unpacked_dtype=jnp.float32)
    return lo, hi


def _memkv_kernel(mem_ref, g_ref, w_ref, o_ref):
    h = _rms(mem_ref[0], g_ref[...]).astype(jnp.bfloat16)
    o_ref[0] = _dot(h, w_ref[...]).astype(jnp.bfloat16)


def _memkv(mem, g_mem, w_mem_kv):
    B, M, D = mem.shape
    return pl.pallas_call(
        _memkv_kernel,
        out_shape=jax.ShapeDtypeStruct((B, M, 2 * XATTN_WIDTH), jnp.bfloat16),
        grid=(B,),
        in_specs=[
            pl.BlockSpec((1, M, D), lambda b: (b, 0, 0)),
            pl.BlockSpec((1, D), lambda b: (0, 0)),
            pl.BlockSpec((D, 2 * XATTN_WIDTH), lambda b: (0, 0)),
        ],
        out_specs=pl.BlockSpec((1, M, 2 * XATTN_WIDTH), lambda b: (b, 0, 0)),
        compiler_params=_cparams(("arbitrary",)),
        name="memkv",
    )(mem, g_mem.reshape(1, D), w_mem_kv.astype(jnp.bfloat16))


def _in_proj_kernel(x_ref, g_ref, w_ref, z_ref):
    h = _rms(x_ref[...], g_ref[...]).astype(jnp.bfloat16)
    z_ref[...] = _dot(h, w_ref[...]).astype(jnp.bfloat16)


def _in_proj(x2d, g, w_bf16):
    T, D = x2d.shape
    N = w_bf16.shape[1]
    return pl.pallas_call(
        _in_proj_kernel,
        out_shape=jax.ShapeDtypeStruct((T, N), jnp.bfloat16),
        grid=(T // TOKEN_TILE,),
        in_specs=[
            pl.BlockSpec((TOKEN_TILE, D), lambda i: (i, 0)),
            pl.BlockSpec((1, D), lambda i: (0, 0)),
            pl.BlockSpec((D, N), lambda i: (0, 0)),
        ],
        out_specs=pl.BlockSpec((TOKEN_TILE, N), lambda i: (i, 0)),
        compiler_params=_cparams(("arbitrary",)),
        name="in_proj",
    )(x2d, g.reshape(1, D), w_bf16)


XATTN_ROWS = 256


def _xattn_rows(qc, mem_k, mem_v):
    lane = lax.broadcasted_iota(jnp.int32, (1, XATTN_WIDTH), 1)
    out = jnp.zeros((qc.shape[0], XATTN_WIDTH), jnp.float32)
    zero = jnp.zeros((), jnp.bfloat16)
    for h in range(XATTN_HEADS):
        hm = (lane >= h * XATTN_HEAD_DIM) & (lane < (h + 1) * XATTN_HEAD_DIM)
        s = _dot_nt(jnp.where(hm, qc, zero), mem_k) * (XATTN_HEAD_DIM ** -0.5)
        m = jnp.max(s, axis=-1, keepdims=True)
        p = jnp.exp(s - m)
        l = jnp.sum(p, axis=-1, keepdims=True)
        o = _dot(p.astype(jnp.bfloat16), jnp.where(hm, mem_v, zero))
        out = out + o / l
    return out


Z_MLA = Q_LORA_RANK + KV_LORA_RANK + 2 * HEAD_BLOCK + XATTN_WIDTH
PROJ_ROWS = 256


def _mla_kernel(z_ref, cs_ref, ec_ref, es_ref, gq_ref, gkv_ref, wqa_ref, wqb_ref, wk_ref, wv_ref, memkv_ref,
                cat_ref, q_s, k_s, v_s, o_s, sc_s):
    hp = pl.program_id(1)
    S = z_ref.shape[1]
    scale = QK_HEAD_DIM ** -0.5 * float(np.log2(np.e))

    @pl.when(hp == 0)
    def _project():
        nope_one = (lax.broadcasted_iota(jnp.int32, (1, HEAD_BLOCK), 1) < QK_NOPE_DIM).astype(jnp.float32)

        def chunk(c, carry):
            rows = pl.ds(pl.multiple_of(c * PROJ_ROWS, PROJ_ROWS), PROJ_ROWS)
            z = z_ref[0, rows, :]
            cs = cs_ref[0, rows, :]
            cs_hi = cs.astype(jnp.bfloat16)
            cs_lo = (cs - cs_hi.astype(jnp.float32)).astype(jnp.bfloat16)
            cos = _dot(cs_hi, ec_ref[...]) + _dot(cs_lo, ec_ref[...]) + nope_one
            sin = _dot(cs_hi, es_ref[...]) + _dot(cs_lo, es_ref[...])
            q_lat = _rms(z[:, :Q_LORA_RANK].astype(jnp.float32), gq_ref[...]).astype(jnp.bfloat16)
            kv_lat = _rms(z[:, Q_LORA_RANK:Q_LORA_RANK + KV_LORA_RANK].astype(jnp.float32),
                          gkv_ref[...]).astype(jnp.bfloat16)
            o0 = Q_LORA_RANK + KV_LORA_RANK
            k_rope = (z[:, o0:o0 + HEAD_BLOCK].astype(jnp.float32) * cos
                      + z[:, o0 + HEAD_BLOCK:o0 + 2 * HEAD_BLOCK].astype(jnp.float32) * sin)
            for hh in range(MLA_HEADS):
                blk = slice(hh * HEAD_BLOCK, (hh + 1) * HEAD_BLOCK)
                qa = _dot(q_lat, wqa_ref[:, blk])
                qb = _dot(q_lat, wqb_ref[:, blk])
                q_s[hh, rows, :] = ((qa * cos + qb * sin) * scale).astype(jnp.bfloat16)
                k_s[hh, rows, :] = (_dot(kv_lat, wk_ref[:, blk]) + k_rope).astype(jnp.bfloat16)
            for p2 in range(MLA_HEADS // 2):
                blk = slice(p2 * HEAD_BLOCK, (p2 + 1) * HEAD_BLOCK)
                v_s[p2, rows, :] = _dot(kv_lat, wv_ref[:, blk]).astype(jnp.bfloat16)
            qc = z[:, o0 + 2 * HEAD_BLOCK:]
            mem_k = memkv_ref[0, :, :XATTN_WIDTH]
            mem_v = memkv_ref[0, :, XATTN_WIDTH:]
            cat_ref[0, rows, TOKEN_MIX_WIDTH:] = _xattn_rows(qc, mem_k, mem_v).astype(jnp.bfloat16)
            return carry

        lax.fori_loop(0, S // PROJ_ROWS, chunk, 0)

    first_half = lax.broadcasted_iota(jnp.int32, (1, HEAD_BLOCK), 1) < V_HEAD_DIM
    tile_lanes = ATTN_TILE // LANES

    def lane_fold(x, op):
        r = x[:, :LANES]
        for c in range(1, tile_lanes):
            r = op(r, x[:, c * LANES:(c + 1) * LANES])
        return r

    def q_block(i, carry0):
        rows = pl.ds(pl.multiple_of(i * ATTN_TILE, ATTN_TILE), ATTN_TILE)
        qs = (q_s[2 * hp, rows, :], q_s[2 * hp + 1, rows, :])

        def scores(c, mrun, masked):
            cols = pl.ds(pl.multiple_of(c * ATTN_TILE, ATTN_TILE), ATTN_TILE)
            out = []
            for t in range(2):
                s = _dot_nt(qs[t], k_s[2 * hp + t, cols, :])
                if masked:
                    q_idx = lax.broadcasted_iota(jnp.int32, s.shape, 0)
                    k_idx = lax.broadcasted_iota(jnp.int32, s.shape, 1)
                    s = jnp.where(k_idx <= q_idx, s, NEG_INF)
                sc_s[t, c] = s
                out.append(jnp.maximum(mrun[t], lane_fold(s, jnp.maximum)))
            return tuple(out)

        neg = jnp.full((ATTN_TILE, LANES), NEG_INF, jnp.float32)
        mrun = lax.fori_loop(0, i, functools.partial(scores, masked=False), (neg, neg))
        mrun = scores(i, mrun, True)
        ms = tuple(jnp.max(m, axis=-1, keepdims=True) for m in mrun)

        def weigh(c, carry):
            cols = pl.ds(pl.multiple_of(c * ATTN_TILE, ATTN_TILE), ATTN_TILE)
            v = v_s[hp, cols, :]
            out = []
            for t in range(2):
                lrun, acc = carry[t]
                p = jnp.exp2(sc_s[t, c] - ms[t])
                out.append((lrun + lane_fold(p, jnp.add), acc + _dot(p.astype(jnp.bfloat16), v)))
            return tuple(out)

        zero = jnp.zeros((ATTN_TILE, LANES), jnp.float32)
        (l0, acc0), (l1, acc1) = lax.fori_loop(0, i + 1, weigh, ((zero, zero), (zero, zero)))
        l0 = jnp.sum(l0, axis=-1, keepdims=True)
        l1 = jnp.sum(l1, axis=-1, keepdims=True)
        o_s[hp, rows, :] = jnp.where(first_half, acc0 / l0, acc1 / l1).astype(jnp.bfloat16)
        return carry0

    lax.fori_loop(0, S // ATTN_TILE, q_block, 0)

    @pl.when(hp == MLA_HEADS // 2 - 1)
    def _emit():
        for p2 in range(MLA_HEADS // 2):
            cat_ref[0, :, p2 * HEAD_BLOCK:(p2 + 1) * HEAD_BLOCK] = o_s[p2]


ROPE_POS_TILE = 2048


def _rope_kernel(pos_ref, invf_ref, cs_ref):
    half = QK_ROPE_DIM // 2
    ang = invf_ref[...] * pos_ref[...].astype(jnp.float32)
    cs_ref[:half, :] = jnp.cos(ang)
    cs_ref[half:, :] = jnp.sin(ang)


def _rope_table(positions):
    B, S = positions.shape
    T = B * S
    half = QK_ROPE_DIM // 2
    inv_freq = ROPE_THETA ** (-np.arange(0, QK_ROPE_DIM, 2, dtype=np.float32) / QK_ROPE_DIM)
    tile = min(ROPE_POS_TILE, T)
    cs = pl.pallas_call(
        _rope_kernel,
        out_shape=jax.ShapeDtypeStruct((QK_ROPE_DIM, T), jnp.float32),
        grid=(T // tile,),
        in_specs=[
            pl.BlockSpec((1, tile), lambda i: (0, i)),
            pl.BlockSpec((half, 1), lambda i: (0, 0)),
        ],
        out_specs=pl.BlockSpec((QK_ROPE_DIM, tile), lambda i: (0, i)),
        compiler_params=_cparams(("arbitrary",)),
        name="rope_table",
    )(positions.reshape(1, T), jnp.asarray(inv_freq.reshape(half, 1)))
    return cs.T.reshape(B, S, QK_ROPE_DIM)


def _mla_layer(z, rope_cs, memkv, g_q, g_kv, w_uq, w_ukv):
    B, S, _ = z.shape
    wq = w_uq.reshape(Q_LORA_RANK, MLA_HEADS, QK_HEAD_DIM)
    nope, x1, x2 = wq[..., :QK_NOPE_DIM], wq[..., QK_NOPE_DIM:QK_NOPE_DIM + 16], wq[..., QK_NOPE_DIM + 16:]
    zpad = jnp.zeros((Q_LORA_RANK, MLA_HEADS, HEAD_BLOCK - QK_HEAD_DIM), w_uq.dtype)
    wqa = jnp.concatenate([nope, x1, x2, zpad], -1).reshape(Q_LORA_RANK, MLA_HEADS * HEAD_BLOCK)
    wqb = jnp.concatenate([jnp.zeros_like(nope), -x2, x1, zpad], -1).reshape(Q_LORA_RANK, MLA_HEADS * HEAD_BLOCK)
    wkv = w_ukv.reshape(KV_LORA_RANK, MLA_HEADS, QK_NOPE_DIM + V_HEAD_DIM)
    wk = jnp.concatenate([wkv[..., :QK_NOPE_DIM], jnp.zeros_like(wkv[..., :QK_NOPE_DIM])], -1)
    wk = wk.reshape(KV_LORA_RANK, MLA_HEADS * HEAD_BLOCK)
    wv = wkv[..., QK_NOPE_DIM:].reshape(KV_LORA_RANK, MLA_HEADS * V_HEAD_DIM)
    half = QK_ROPE_DIM // 2
    ec = np.zeros((QK_ROPE_DIM, HEAD_BLOCK), np.float32)
    es = np.zeros((QK_ROPE_DIM, HEAD_BLOCK), np.float32)
    for f in range(half):
        ec[f, QK_NOPE_DIM + f] = ec[f, QK_NOPE_DIM + half + f] = 1.0
        es[half + f, QK_NOPE_DIM + f] = es[half + f, QK_NOPE_DIM + half + f] = 1.0
    bf = jnp.bfloat16
    const = lambda b, h: (0, 0)
    return pl.pallas_call(
        _mla_kernel,
        out_shape=jax.ShapeDtypeStruct((B, S, D_MODEL), jnp.bfloat16),
        grid=(B, MLA_HEADS // 2),
        in_specs=[
            pl.BlockSpec((1, S, Z_MLA), lambda b, h: (b, 0, 0)),
            pl.BlockSpec((1, S, QK_ROPE_DIM), lambda b, h: (b, 0, 0)),
            pl.BlockSpec((QK_ROPE_DIM, HEAD_BLOCK), const),
            pl.BlockSpec((QK_ROPE_DIM, HEAD_BLOCK), const),
            pl.BlockSpec((1, Q_LORA_RANK), const),
            pl.BlockSpec((1, KV_LORA_RANK), const),
            pl.BlockSpec((Q_LORA_RANK, MLA_HEADS * HEAD_BLOCK), const),
            pl.BlockSpec((Q_LORA_RANK, MLA_HEADS * HEAD_BLOCK), const),
            pl.BlockSpec((KV_LORA_RANK, MLA_HEADS * HEAD_BLOCK), const),
            pl.BlockSpec((KV_LORA_RANK, MLA_HEADS * V_HEAD_DIM), const),
            pl.BlockSpec((1, MEM_LEN, 2 * XATTN_WIDTH), lambda b, h: (b, 0, 0)),
        ],
        out_specs=pl.BlockSpec((1, S, D_MODEL), lambda b, h: (b, 0, 0)),
        scratch_shapes=[
            pltpu.VMEM((MLA_HEADS, S, HEAD_BLOCK), bf),
            pltpu.VMEM((MLA_HEADS, S, HEAD_BLOCK), bf),
            pltpu.VMEM((MLA_HEADS // 2, S, HEAD_BLOCK), bf),
            pltpu.VMEM((MLA_HEADS // 2, S, HEAD_BLOCK), bf),
            pltpu.VMEM((2, S // ATTN_TILE, ATTN_TILE, ATTN_TILE), jnp.float32),
        ],
        compiler_params=_cparams(("arbitrary", "arbitrary")),
        name="mla_mixer",
    )(z, rope_cs, jnp.asarray(ec, bf), jnp.asarray(es, bf), g_q.reshape(1, -1), g_kv.reshape(1, -1),
      wqa.astype(bf), wqb.astype(bf), wk.astype(bf), wv.astype(bf), memkv)


def _mla_in_weight(w_in):
    o0 = Q_LORA_RANK + KV_LORA_RANK
    x1, x2 = w_in[:, o0:o0 + 16], w_in[:, o0 + 16:o0 + 32]
    z64 = jnp.zeros((D_MODEL, QK_NOPE_DIM), w_in.dtype)
    z32 = jnp.zeros((D_MODEL, HEAD_BLOCK - QK_HEAD_DIM), w_in.dtype)
    return jnp.concatenate([w_in[:, :o0], z64, x1, x2, z32, z64, -x2, x1, z32, w_in[:, o0 + 32:]], axis=1)


def _shift_rows(x, k):
    rolled = pltpu.roll(x, k, 0)
    row = lax.broadcasted_iota(jnp.int32, x.shape, 0)
    return jnp.where(row >= k, rolled, 0.0)


def _pool_kernel(z_ref, wmix_ref, scale_ref, memkv_ref, cat_ref, d_s):
    S = z_ref.shape[1]
    step = (lax.broadcasted_iota(jnp.int32, (S, 1), 0) + 1).astype(jnp.float32)
    for ct in range(TOKEN_MIX_WIDTH // LANES):
        cols = slice(ct * LANES, (ct + 1) * LANES)
        u = z_ref[0, :, cols].astype(jnp.float32)
        groups = sorted({(ct * LANES) // POOL_GROUP_WIDTH, (ct * LANES + LANES - 1) // POOL_GROUP_WIDTH})
        sums, cur, w = {}, u, 1
        while w < POOL_WINDOWS[groups[-1]]:
            cur = cur + _shift_rows(cur, w)
            w *= 2
            sums[w] = cur
        mean = sums[POOL_WINDOWS[groups[0]]] / jnp.minimum(step, float(POOL_WINDOWS[groups[0]]))
        if len(groups) == 2:
            win = POOL_WINDOWS[groups[1]]
            col = ct * LANES + lax.broadcasted_iota(jnp.int32, (1, LANES), 1)
            mean = jnp.where(col < groups[1] * POOL_GROUP_WIDTH, mean, sums[win] / jnp.minimum(step, float(win)))
        d_s[:, cols] = (mean - u).astype(jnp.bfloat16)
    y = _dot(d_s[...], wmix_ref[...]) * scale_ref[...]
    cat_ref[0, :, :TOKEN_MIX_WIDTH] = y.astype(jnp.bfloat16)
    mem_k = memkv_ref[0, :, :XATTN_WIDTH]
    mem_v = memkv_ref[0, :, XATTN_WIDTH:]
    for xc in range(S // XATTN_ROWS):
        rows = pl.ds(xc * XATTN_ROWS, XATTN_ROWS)
        qc = z_ref[0, rows, TOKEN_MIX_WIDTH:]
        cat_ref[0, rows, TOKEN_MIX_WIDTH:] = _xattn_rows(qc, mem_k, mem_v).astype(jnp.bfloat16)


def _pool_layer(z, memkv, w_pool_mix, pool_scale):
    B, S, _ = z.shape
    wmix = jnp.zeros((TOKEN_MIX_WIDTH, TOKEN_MIX_WIDTH), w_pool_mix.dtype)
    for g in range(len(POOL_WINDOWS)):
        blk = slice(g * POOL_GROUP_WIDTH, (g + 1) * POOL_GROUP_WIDTH)
        wmix = wmix.at[blk, blk].set(w_pool_mix[g])
    return pl.pallas_call(
        _pool_kernel,
        out_shape=jax.ShapeDtypeStruct((B, S, D_MODEL), jnp.bfloat16),
        grid=(B,),
        in_specs=[
            pl.BlockSpec((1, S, D_MODEL), lambda b: (b, 0, 0)),
            pl.BlockSpec((TOKEN_MIX_WIDTH, TOKEN_MIX_WIDTH), lambda b: (0, 0)),
            pl.BlockSpec((1, TOKEN_MIX_WIDTH), lambda b: (0, 0)),
            pl.BlockSpec((1, MEM_LEN, 2 * XATTN_WIDTH), lambda b: (b, 0, 0)),
        ],
        out_specs=pl.BlockSpec((1, S, D_MODEL), lambda b: (b, 0, 0)),
        scratch_shapes=[pltpu.VMEM((S, TOKEN_MIX_WIDTH), jnp.bfloat16)],
        compiler_params=_cparams(("arbitrary",)),
        name="pool_mixer",
    )(z, wmix.astype(jnp.bfloat16), pool_scale.reshape(1, -1), memkv)


def _post_kernel(cat_ref, x_ref, wo_ref, g_ref, wr_hi_ref, wr_lo_ref, br_ref,
                 x1_ref, hp_ref, cls_ref, cnt_ref):
    i = pl.program_id(0)
    tm = x_ref.shape[0]
    x1 = x_ref[...] + _dot(cat_ref[...], wo_ref[...])
    x1_ref[...] = x1
    hn = _rms(x1, g_ref[...])
    hi = hn.astype(jnp.bfloat16)
    lo = (hn - hi.astype(jnp.float32)).astype(jnp.bfloat16)
    logits = (_dot(hi, wr_hi_ref[...]) + _dot(lo, wr_hi_ref[...]) + _dot(hi, wr_lo_ref[...])
              + br_ref[...])
    lane = lax.broadcasted_iota(jnp.int32, (tm, LANES), 1)
    lane_f = lane.astype(jnp.float32)

    def first_argmax(v):
        m = jnp.max(v, axis=-1, keepdims=True)
        idx = jnp.min(jnp.where(v == m, lane_f, float(LANES)), axis=-1, keepdims=True)
        return m, idx.astype(jnp.int32)

    is_g = lane < N_EXPERT_GROUPS
    gmax, g_sel = first_argmax(jnp.where(is_g, logits, NEG_INF))
    gate_g = 1.0 / jnp.sum(jnp.where(is_g, jnp.exp(logits - gmax), 0.0), axis=-1, keepdims=True)
    lo_lane = N_EXPERT_GROUPS + EXPERTS_PER_GROUP * g_sel
    le = jnp.where((lane >= lo_lane) & (lane < lo_lane + EXPERTS_PER_GROUP), logits, NEG_INF)
    m1, i1 = first_argmax(le)
    m2, i2 = first_argmax(jnp.where(lane == i1, NEG_INF, le))
    e2 = jnp.exp(m2 - m1)
    w1 = gate_g / (1.0 + e2)
    w2 = gate_g * e2 / (1.0 + e2)
    j1, j2 = i1 - lo_lane, i2 - lo_lane
    first_low = j1 < j2
    a = jnp.where(first_low, j1, j2)
    b = jnp.where(first_low, j2, j1)
    w_a = jnp.where(first_low, w1, w2)
    w_b = jnp.where(first_low, w2, w1)
    cls = g_sel * PAIRS_PER_GROUP + ((a * (15 - a)) >> 1) + (b - a - 1)

    hp_ref[:, :HALF] = _pack_halves(hn)
    wa_bits = _bits(jnp.broadcast_to(w_a, (tm, LANES)))
    wb_bits = _bits(jnp.broadcast_to(w_b, (tm, LANES)))
    hp_ref[:, HALF:] = jnp.where(lane == 0, wa_bits, jnp.where(lane == 1, wb_bits, jnp.uint32(0)))

    cls_l0 = jnp.where(lane == 0, cls.astype(jnp.float32), 0.0).astype(jnp.bfloat16)
    sel = (lax.broadcasted_iota(jnp.int32, (8, LANES), 1) == 0).astype(jnp.bfloat16)
    cls_row = _dot_nt(sel, cls_l0)
    cls_ref[0] = cls_row.astype(jnp.int32)

    cid = lax.broadcasted_iota(jnp.int32, (CLASS_PAD, tm), 0).astype(jnp.float32)
    onehot_t = (cid == jnp.broadcast_to(cls_row[0:1, :], (CLASS_PAD, tm))).astype(jnp.float32)
    tile_cnt = jnp.broadcast_to(jnp.sum(onehot_t, axis=-1, keepdims=True), (CLASS_PAD, LANES))

    @pl.when(i == 0)
    def _init():
        cnt_ref[...] = tile_cnt

    @pl.when(i > 0)
    def _acc():
        cnt_ref[...] = cnt_ref[...] + tile_cnt


def _post(cat2d, x2d, w_o, g_ffn, w_rg, b_rg, w_re, b_re):
    T, D = x2d.shape
    n = T // TOKEN_TILE
    wr = jnp.zeros((D, LANES), jnp.float32)
    wr = wr.at[:, :N_EXPERT_GROUPS].set(w_rg).at[:, N_EXPERT_GROUPS:N_EXPERT_GROUPS + N_EXPERTS].set(w_re)
    wr_hi = wr.astype(jnp.bfloat16)
    wr_lo = (wr - wr_hi.astype(jnp.float32)).astype(jnp.bfloat16)
    br = jnp.zeros((1, LANES), jnp.float32)
    br = br.at[0, :N_EXPERT_GROUPS].set(b_rg).at[0, N_EXPERT_GROUPS:N_EXPERT_GROUPS + N_EXPERTS].set(b_re)
    const = lambda i: (0, 0)
    return pl.pallas_call(
        _post_kernel,
        out_shape=(
            jax.ShapeDtypeStruct((T, D), jnp.float32),
            jax.ShapeDtypeStruct((T, PACKED_ROW), jnp.uint32),
            jax.ShapeDtypeStruct((n, 8, TOKEN_TILE), jnp.int32),
            jax.ShapeDtypeStruct((CLASS_PAD, LANES), jnp.float32),
        ),
        grid=(n,),
        in_specs=[
            pl.BlockSpec((TOKEN_TILE, D), lambda i: (i, 0)),
            pl.BlockSpec((TOKEN_TILE, D), lambda i: (i, 0)),
            pl.BlockSpec((D, D), const),
            pl.BlockSpec((1, D), const),
            pl.BlockSpec((D, LANES), const),
            pl.BlockSpec((D, LANES), const),
            pl.BlockSpec((1, LANES), const),
        ],
        out_specs=(
            pl.BlockSpec((TOKEN_TILE, D), lambda i: (i, 0)),
            pl.BlockSpec((TOKEN_TILE, PACKED_ROW), lambda i: (i, 0)),
            pl.BlockSpec((1, 8, TOKEN_TILE), lambda i: (i, 0, 0)),
            pl.BlockSpec((CLASS_PAD, LANES), const),
        ),
        compiler_params=_cparams(("arbitrary",)),
        name="out_proj_router",
    )(cat2d, x2d, w_o.astype(jnp.bfloat16), g_ffn.reshape(1, D), wr_hi, wr_lo, br)


def _max_expert_tiles(T):
    return T // EXPERT_TILE + N_CLASSES


def _slots_kernel(cls_ref, cnt_ref, tri_ref, ltri_ref, pos_ref, tcls_ref, base_ref):
    i = pl.program_id(0)
    tm = cls_ref.shape[2]
    n_lane = tcls_ref.shape[1]

    @pl.when(i == 0)
    def _offsets():
        tiles = jnp.floor((cnt_ref[...] + (EXPERT_TILE - 1)) * (1.0 / EXPERT_TILE))
        before = _dot(ltri_ref[...], tiles.astype(jnp.bfloat16))
        base_ref[...] = before * EXPERT_TILE
        upto = (before + tiles)[:, 0:1]
        k = lax.broadcasted_iota(jnp.int32, (CLASS_PAD, n_lane), 1).astype(jnp.float32)
        cid = lax.broadcasted_iota(jnp.int32, (CLASS_PAD, n_lane), 0)
        done = jnp.where((upto <= k) & (cid < N_CLASSES), 1.0, 0.0)
        tile_cls = jnp.sum(done, axis=0, keepdims=True)
        total = jnp.max(jnp.where(cid < N_CLASSES, jnp.broadcast_to(upto, (CLASS_PAD, n_lane)), 0.0),
                        axis=0, keepdims=True)
        row = lax.broadcasted_iota(jnp.int32, (8, n_lane), 0)
        tcls_ref[...] = jnp.where(row == 0, jnp.broadcast_to(tile_cls, (8, n_lane)),
                                  jnp.broadcast_to(total, (8, n_lane))).astype(jnp.int32)

    cls_row = cls_ref[0, 0:1, :]
    cid = lax.broadcasted_iota(jnp.int32, (CLASS_PAD, tm), 0)
    onehot_t = cid == jnp.broadcast_to(cls_row, (CLASS_PAD, tm))
    oh = onehot_t.astype(jnp.bfloat16)
    earlier = _dot(oh, tri_ref[...])
    base = base_ref[:, 0:1]
    slot = jnp.sum(jnp.where(onehot_t, earlier + base, 0.0), axis=0, keepdims=True)
    pos_ref[0] = slot.astype(jnp.int32)
    base_ref[...] = base_ref[...] + jnp.sum(onehot_t.astype(jnp.float32), axis=-1, keepdims=True)


def _slots(cls, cnt):
    n, _, tm = cls.shape
    T = n * tm
    n_lane = pl.cdiv(_max_expert_tiles(T), LANES) * LANES
    tri = jnp.asarray(np.triu(np.ones((tm, tm), np.float32), 1), jnp.bfloat16)
    ltri = jnp.asarray(np.tril(np.ones((CLASS_PAD, CLASS_PAD), np.float32), -1), jnp.bfloat16)
    return pl.pallas_call(
        _slots_kernel,
        out_shape=(
            jax.ShapeDtypeStruct((n, 1, tm), jnp.int32),
            jax.ShapeDtypeStruct((8, n_lane), jnp.int32),
        ),
        grid=(n,),
        in_specs=[
            pl.BlockSpec((1, 8, tm), lambda i: (i, 0, 0)),
            pl.BlockSpec((CLASS_PAD, LANES), lambda i: (0, 0)),
            pl.BlockSpec((tm, tm), lambda i: (0, 0)),
            pl.BlockSpec((CLASS_PAD, CLASS_PAD), lambda i: (0, 0)),
        ],
        out_specs=(
            pl.BlockSpec((1, 1, tm), lambda i: (i, 0, 0)),
            pl.BlockSpec((8, n_lane), lambda i: (0, 0)),
        ),
        scratch_shapes=[pltpu.VMEM((CLASS_PAD, LANES), jnp.float32)],
        compiler_params=_cparams(("arbitrary",)),
        name="moe_slots",
    )(cls, cnt, tri, ltri)


def _row_copy(src_ref, src_row, dst_ref, dst_row, sem):
    return pltpu.make_async_copy(src_ref.at[pl.ds(src_row, 1), :], dst_ref.at[pl.ds(dst_row, 1), :], sem)


def _dispatch_kernel(pos_ref, hp_ref, xs_in_ref, xs_ref, sem):
    del xs_in_ref
    tm = hp_ref.shape[0]

    def start(r, c):
        _row_copy(hp_ref, r, xs_ref, pos_ref[0, 0, r], sem).start()
        return c

    lax.fori_loop(0, tm, start, 0, unroll=8)

    def wait(r, c):
        _row_copy(hp_ref, 0, xs_ref, 0, sem).wait()
        return c

    lax.fori_loop(0, tm, wait, 0, unroll=8)


def _dispatch(pos, hp, n_rows):
    T, W = hp.shape
    n = T // TOKEN_TILE
    return pl.pallas_call(
        _dispatch_kernel,
        out_shape=jax.ShapeDtypeStruct((n_rows, W), jnp.uint32),
        grid=(n,),
        in_specs=[
            pl.BlockSpec((1, 1, TOKEN_TILE), lambda i: (i, 0, 0), memory_space=pltpu.SMEM),
            pl.BlockSpec((TOKEN_TILE, W), lambda i: (i, 0)),
            pl.BlockSpec(memory_space=pl.ANY),
        ],
        out_specs=pl.BlockSpec(memory_space=pl.ANY),
        scratch_shapes=[pltpu.SemaphoreType.DMA],
        input_output_aliases={2: 0},
        compiler_params=_cparams(("arbitrary",)),
        name="moe_dispatch",
    )(pos, hp, jnp.zeros((n_rows, W), jnp.uint32))


def _expert_kernel(ea_ref, eb_ref, nt_ref, xs_ref, wg_a_ref, wu_a_ref, wd_a_ref, wg_b_ref, wu_b_ref, wd_b_ref,
                   ys_ref, wgu_a_s, wd_a_s, wgu_b_s, wd_b_s):
    i = pl.program_id(0)
    prev = jnp.maximum(i - 1, 0)

    def refresh(e_ref, wg_ref, wu_ref, wd_ref, wgu_s, wd_s):
        @pl.when((i == 0) | (e_ref[i] != e_ref[prev]))
        def _cast():
            wgu_s[:, :D_EXPERT] = wg_ref[...].astype(jnp.bfloat16)
            wgu_s[:, D_EXPERT:] = wu_ref[...].astype(jnp.bfloat16)
            wd_s[...] = wd_ref[...].astype(jnp.bfloat16)

    refresh(ea_ref, wg_a_ref, wu_a_ref, wd_a_ref, wgu_a_s, wd_a_s)
    refresh(eb_ref, wg_b_ref, wu_b_ref, wd_b_ref, wgu_b_s, wd_b_s)

    @pl.when(i < nt_ref[0])
    def _run():
        lo, hi = _unpack_halves(xs_ref[:, :HALF])
        lo = lo.astype(jnp.bfloat16)
        hi = hi.astype(jnp.bfloat16)
        gates = lax.bitcast_convert_type(xs_ref[:, HALF:], jnp.float32)
        y = None
        for wgu_ref, wd_ref, lane in ((wgu_a_s, wd_a_s, 0), (wgu_b_s, wd_b_s, 1)):
            gu = _dot(lo, wgu_ref[:HALF, :]) + _dot(hi, wgu_ref[HALF:, :])
            g, u = gu[:, :D_EXPERT], gu[:, D_EXPERT:]
            act = (g * jax.nn.sigmoid(g)) * u * gates[:, lane:lane + 1]
            part = _dot(act.astype(jnp.bfloat16), wd_ref[...])
            y = part if y is None else y + part
        ys_ref[...] = _pack_halves(y)

    @pl.when(i >= nt_ref[0])
    def _unused():
        ys_ref[...] = jnp.zeros(ys_ref.shape, ys_ref.dtype)


def _experts(xs, tile_ea, tile_eb, n_tiles, w_gate, w_up, w_down):
    n_rows = xs.shape[0]
    max_tiles = n_rows // EXPERT_TILE
    bf = jnp.bfloat16

    def row_map(i, ea, eb, nt):
        return (jnp.minimum(i, nt[0] - 1), 0)

    in_a = lambda i, ea, eb, nt: (ea[i], 0, 0)
    in_b = lambda i, ea, eb, nt: (eb[i], 0, 0)
    up_spec = lambda m: pl.BlockSpec((None, D_MODEL, D_EXPERT), m)
    down_spec = lambda m: pl.BlockSpec((None, D_EXPERT, D_MODEL), m)
    return pl.pallas_call(
        _expert_kernel,
        out_shape=jax.ShapeDtypeStruct((n_rows, HALF), jnp.uint32),
        grid_spec=pltpu.PrefetchScalarGridSpec(
            num_scalar_prefetch=3,
            grid=(max_tiles,),
            in_specs=[
                pl.BlockSpec((EXPERT_TILE, PACKED_ROW), row_map),
                up_spec(in_a), up_spec(in_a), down_spec(in_a),
                up_spec(in_b), up_spec(in_b), down_spec(in_b),
            ],
            out_specs=pl.BlockSpec((EXPERT_TILE, HALF), lambda i, ea, eb, nt: (i, 0)),
            scratch_shapes=[
                pltpu.VMEM((D_MODEL, 2 * D_EXPERT), bf), pltpu.VMEM((D_EXPERT, D_MODEL), bf),
                pltpu.VMEM((D_MODEL, 2 * D_EXPERT), bf), pltpu.VMEM((D_EXPERT, D_MODEL), bf),
            ],
        ),
        compiler_params=_cparams(("arbitrary",)),
        name="moe_experts",
    )(tile_ea, tile_eb, n_tiles, xs, w_gate, w_up, w_down, w_gate, w_up, w_down)


def _gathered_rows(pos_ref, pos_next_ref, ys_ref, ybuf, sems):
    i = pl.program_id(0)
    n = pl.num_programs(0)
    tm = ybuf.shape[1]
    slot = i % 2

    def start_tile(p_ref, dst_slot):
        def start(r, c):
            _row_copy(ys_ref, p_ref[0, 0, r], ybuf.at[dst_slot], r, sems.at[dst_slot]).start()
            return c

        lax.fori_loop(0, tm, start, 0, unroll=8)

    @pl.when(i == 0)
    def _prime():
        start_tile(pos_ref, 0)

    @pl.when(i + 1 < n)
    def _ahead():
        start_tile(pos_next_ref, 1 - slot)

    def wait(r, c):
        _row_copy(ys_ref, 0, ybuf.at[slot], 0, sems.at[slot]).wait()
        return c

    lax.fori_loop(0, tm, wait, 0, unroll=8)
    lo, hi = _unpack_halves(ybuf[slot])
    return jnp.concatenate([lo, hi], axis=1)


def _combine_proj_kernel(pos_ref, pos_next_ref, x1_ref, ys_ref, g_ref, w_ref, x2_ref, z_ref, ybuf, sems):
    x2 = x1_ref[...] + _gathered_rows(pos_ref, pos_next_ref, ys_ref, ybuf, sems)
    x2_ref[...] = x2
    h = _rms(x2, g_ref[...]).astype(jnp.bfloat16)
    z_ref[...] = _dot(h, w_ref[...]).astype(jnp.bfloat16)


def _combine_final_kernel(pos_ref, pos_next_ref, x1_ref, ys_ref, g_ref, out_ref, ybuf, sems):
    out_ref[...] = _rms(x1_ref[...] + _gathered_rows(pos_ref, pos_next_ref, ys_ref, ybuf, sems), g_ref[...])


def _combine(pos, x1, ys, g, w_next=None):
    T, D = x1.shape
    n = T // TOKEN_TILE
    in_specs = [
        pl.BlockSpec((1, 1, TOKEN_TILE), lambda i: (i, 0, 0), memory_space=pltpu.SMEM),
        pl.BlockSpec((1, 1, TOKEN_TILE), lambda i: (jnp.minimum(i + 1, n - 1), 0, 0), memory_space=pltpu.SMEM),
        pl.BlockSpec((TOKEN_TILE, D), lambda i: (i, 0)),
        pl.BlockSpec(memory_space=pl.ANY),
        pl.BlockSpec((1, D), lambda i: (0, 0)),
    ]
    scratch = [pltpu.VMEM((2, TOKEN_TILE, HALF), jnp.uint32), pltpu.SemaphoreType.DMA((2,))]
    row_spec = pl.BlockSpec((TOKEN_TILE, D), lambda i: (i, 0))
    if w_next is None:
        return pl.pallas_call(
            _combine_final_kernel,
            out_shape=jax.ShapeDtypeStruct((T, D), jnp.float32),
            grid=(n,), in_specs=in_specs, out_specs=row_spec, scratch_shapes=scratch,
            compiler_params=_cparams(("arbitrary",)), name="moe_combine_final",
        )(pos, pos, x1, ys, g.reshape(1, D))
    N = w_next.shape[1]
    return pl.pallas_call(
        _combine_proj_kernel,
        out_shape=(jax.ShapeDtypeStruct((T, D), jnp.float32), jax.ShapeDtypeStruct((T, N), jnp.bfloat16)),
        grid=(n,),
        in_specs=in_specs + [pl.BlockSpec((D, N), lambda i: (0, 0))],
        out_specs=(row_spec, pl.BlockSpec((TOKEN_TILE, N), lambda i: (i, 0))),
        scratch_shapes=scratch,
        compiler_params=_cparams(("arbitrary",)), name="moe_combine_proj",
    )(pos, pos, x1, ys, g.reshape(1, D), w_next)


def _moe_sorted(cat2d, x2d, w_o, g_ffn, w_rg, b_rg, w_re, b_re, w_gate, w_up, w_down):
    T = x2d.shape[0]
    x1, hp, cls, cnt = _post(cat2d, x2d, w_o, g_ffn, w_rg, b_rg, w_re, b_re)
    pos, tcls = _slots(cls, cnt)
    max_tiles = _max_expert_tiles(T)
    n_tiles = tcls[1, 0:1]
    tile_cls = jnp.minimum(tcls[0, :max_tiles], tcls[0, jnp.maximum(n_tiles[0] - 1, 0)])
    tile_ea = jnp.asarray(_CLASS_EA)[tile_cls]
    tile_eb = jnp.asarray(_CLASS_EB)[tile_cls]
    xs = _dispatch(pos, hp, max_tiles * EXPERT_TILE)
    ys = _experts(xs, tile_ea, tile_eb, n_tiles, w_gate, w_up, w_down)
    return x1, pos, ys


def kernel(x, mem, positions, g_mix, w_in_mla, g_q_lat, w_uq, g_kv_lat, w_ukv, w_in_pool, w_pool_mix, pool_scale, g_mem, w_mem_kv, w_o, g_ffn, w_router_group, b_router_group, w_router_expert, b_router_expert, w_expert_gate, w_expert_up, w_expert_down, g_final):
    B, S, D = x.shape
    T = B * S
    depth = g_mix.shape[0]
    bf = jnp.bfloat16
    memkv = _memkv(mem, g_mem, w_mem_kv)
    rope_cs = _rope_table(positions)
    x2d = x.reshape(T, D)
    z = _in_proj(x2d, g_mix[0], _mla_in_weight(w_in_mla[0]).astype(bf))
    for i in range(depth):
        j = i // 2
        if i % 2 == 0:
            cat = _mla_layer(z.reshape(B, S, -1), rope_cs, memkv, g_q_lat[j], g_kv_lat[j], w_uq[j], w_ukv[j])
        else:
            cat = _pool_layer(z.reshape(B, S, -1), memkv, w_pool_mix[j], pool_scale[j])
        x1, pos, ys = _moe_sorted(
            cat.reshape(T, D), x2d, w_o[i], g_ffn[i], w_router_group[i], b_router_group[i],
            w_router_expert[i], b_router_expert[i], w_expert_gate[i], w_expert_up[i], w_expert_down[i])
        if i + 1 == depth:
            return _combine(pos, x1, ys, g_final).reshape(B, S, D)
        nxt = i + 1
        w_next = (_mla_in_weight(w_in_mla[nxt // 2]) if nxt % 2 == 0 else w_in_pool[nxt // 2]).astype(bf)
        x2d, z = _combine(pos, x1, ys, g_mix[nxt], w_next)
```

```python
import functools

import jax
import jax.numpy as jnp
import numpy as np
from jax import lax
from jax.experimental import pallas as pl
from jax.experimental.pallas import tpu as pltpu

D_MODEL = 1024
EPS = 1e-6
NEG_INF = -1e30
MEM_LEN = 256
XATTN_HEADS = 4
XATTN_WIDTH = 256
XATTN_HEAD_DIM = 64
TOKEN_MIX_WIDTH = 768
QK_NOPE_DIM = 64
QK_ROPE_DIM = 32
QK_HEAD_DIM = 96
V_HEAD_DIM = 64
MLA_HEADS = 12
Q_LORA_RANK = 256
KV_LORA_RANK = 128
ROPE_THETA = 10000.0
POOL_WINDOWS = (2, 4, 8, 16)
POOL_GROUP_WIDTH = 192
N_EXPERT_GROUPS = 4
EXPERTS_PER_GROUP = 8
N_EXPERTS = 32
D_EXPERT = 256

LANES = 128
VMEM_LIMIT_BYTES = 56 * 1024 * 1024

TOKEN_TILE = 1024
EXPERT_TILE = 256
ATTN_TILE = 512
HEAD_BLOCK = LANES

PAIRS_PER_GROUP = EXPERTS_PER_GROUP * (EXPERTS_PER_GROUP - 1) // 2
N_CLASSES = N_EXPERT_GROUPS * PAIRS_PER_GROUP
CLASS_PAD = LANES
HALF = D_MODEL // 2
PACKED_ROW = HALF + LANES


def _class_tables():
    ea = np.zeros((CLASS_PAD,), np.int32)
    eb = np.zeros((CLASS_PAD,), np.int32)
    for g in range(N_EXPERT_GROUPS):
        for a in range(EXPERTS_PER_GROUP):
            for b in range(a + 1, EXPERTS_PER_GROUP):
                c = g * PAIRS_PER_GROUP + (a * (15 - a)) // 2 + (b - a - 1)
                ea[c] = g * EXPERTS_PER_GROUP + a
                eb[c] = g * EXPERTS_PER_GROUP + b
    return ea, eb


_CLASS_EA, _CLASS_EB = _class_tables()


def _cparams(sem):
    return pltpu.CompilerParams(dimension_semantics=sem, vmem_limit_bytes=VMEM_LIMIT_BYTES)


def _rms(x, g):
    return x * lax.rsqrt(jnp.mean(x * x, axis=-1, keepdims=True) + EPS) * g


def _dot(a, b):
    return jnp.dot(a, b, preferred_element_type=jnp.float32)


def _dot_nt(a, b):
    return lax.dot_general(a, b, (((1,), (1,)), ((), ())), preferred_element_type=jnp.float32)


def _bits(x):
    return lax.bitcast_convert_type(x, jnp.uint32)


def _pack_halves(y):
    return pltpu.pack_elementwise([y[:, :HALF], y[:, HALF:]], packed_dtype=jnp.bfloat16)


def _unpack_halves(w):
    lo = pltpu.unpack_elementwise(w, index=0, packed_dtype=jnp.bfloat16, unpacked_dtype=jnp.float32)
    hi = pltpu.unpack_elementwise(w, index=1, packed_dtype=jnp.bfloat16, unpacked_dtype=jnp.float32)
    return lo, hi


def _memkv_kernel(mem_ref, g_ref, w_ref, o_ref):
    h = _rms(mem_ref[0], g_ref[...]).astype(jnp.bfloat16)
    o_ref[0] = _dot(h, w_ref[...]).astype(jnp.bfloat16)


def _memkv(mem, g_mem, w_mem_kv):
    B, M, D = mem.shape
    return pl.pallas_call(
        _memkv_kernel,
        out_shape=jax.ShapeDtypeStruct((B, M, 2 * XATTN_WIDTH), jnp.bfloat16),
        grid=(B,),
        in_specs=[
            pl.BlockSpec((1, M, D), lambda b: (b, 0, 0)),
            pl.BlockSpec((1, D), lambda b: (0, 0)),
            pl.BlockSpec((D, 2 * XATTN_WIDTH), lambda b: (0, 0)),
        ],
        out_specs=pl.BlockSpec((1, M, 2 * XATTN_WIDTH), lambda b: (b, 0, 0)),
        compiler_params=_cparams(("arbitrary",)),
        name="memkv",
    )(mem, g_mem.reshape(1, D), w_mem_kv.astype(jnp.bfloat16))


def _in_proj_kernel(x_ref, g_ref, w_ref, z_ref):
    h = _rms(x_ref[...], g_ref[...]).astype(jnp.bfloat16)
    z_ref[...] = _dot(h, w_ref[...]).astype(jnp.bfloat16)


def _in_proj(x2d, g, w_bf16):
    T, D = x2d.shape
    N = w_bf16.shape[1]
    return pl.pallas_call(
        _in_proj_kernel,
        out_shape=jax.ShapeDtypeStruct((T, N), jnp.bfloat16),
        grid=(T // TOKEN_TILE,),
        in_specs=[
            pl.BlockSpec((TOKEN_TILE, D), lambda i: (i, 0)),
            pl.BlockSpec((1, D), lambda i: (0, 0)),
            pl.BlockSpec((D, N), lambda i: (0, 0)),
        ],
        out_specs=pl.BlockSpec((TOKEN_TILE, N), lambda i: (i, 0)),
        compiler_params=_cparams(("arbitrary",)),
        name="in_proj",
    )(x2d, g.reshape(1, D), w_bf16)


XATTN_ROWS = 256


def _xattn_rows(qc, mem_k, mem_v):
    lane = lax.broadcasted_iota(jnp.int32, (1, XATTN_WIDTH), 1)
    out = jnp.zeros((qc.shape[0], XATTN_WIDTH), jnp.float32)
    zero = jnp.zeros((), jnp.bfloat16)
    for h in range(XATTN_HEADS):
        hm = (lane >= h * XATTN_HEAD_DIM) & (lane < (h + 1) * XATTN_HEAD_DIM)
        s = _dot_nt(jnp.where(hm, qc, zero), mem_k) * (XATTN_HEAD_DIM ** -0.5)
        m = jnp.max(s, axis=-1, keepdims=True)
        p = jnp.exp(s - m)
        l = jnp.sum(p, axis=-1, keepdims=True)
        o = _dot(p.astype(jnp.bfloat16), jnp.where(hm, mem_v, zero))
        out = out + o / l
    return out


Z_MLA = Q_LORA_RANK + KV_LORA_RANK + 2 * HEAD_BLOCK + XATTN_WIDTH
PROJ_ROWS = 256


def _mla_kernel(z_ref, cs_ref, ec_ref, es_ref, gq_ref, gkv_ref, wqa_ref, wqb_ref, wk_ref, wv_ref, memkv_ref,
                cat_ref, q_s, k_s, v_s, o_s, sc_s):
    hp = pl.program_id(1)
    S = z_ref.shape[1]
    scale = QK_HEAD_DIM ** -0.5 * float(np.log2(np.e))

    @pl.when(hp == 0)
    def _project():
        nope_one = (lax.broadcasted_iota(jnp.int32, (1, HEAD_BLOCK), 1) < QK_NOPE_DIM).astype(jnp.float32)

        def chunk(c, carry):
            rows = pl.ds(pl.multiple_of(c * PROJ_ROWS, PROJ_ROWS), PROJ_ROWS)
            z = z_ref[0, rows, :]
            cs = cs_ref[0, rows, :]
            cs_hi = cs.astype(jnp.bfloat16)
            cs_lo = (cs - cs_hi.astype(jnp.float32)).astype(jnp.bfloat16)
            cos = _dot(cs_hi, ec_ref[...]) + _dot(cs_lo, ec_ref[...]) + nope_one
            sin = _dot(cs_hi, es_ref[...]) + _dot(cs_lo, es_ref[...])
            q_lat = _rms(z[:, :Q_LORA_RANK].astype(jnp.float32), gq_ref[...]).astype(jnp.bfloat16)
            kv_lat = _rms(z[:, Q_LORA_RANK:Q_LORA_RANK + KV_LORA_RANK].astype(jnp.float32),
                          gkv_ref[...]).astype(jnp.bfloat16)
            o0 = Q_LORA_RANK + KV_LORA_RANK
            k_rope = (z[:, o0:o0 + HEAD_BLOCK].astype(jnp.float32) * cos
                      + z[:, o0 + HEAD_BLOCK:o0 + 2 * HEAD_BLOCK].astype(jnp.float32) * sin)
            for hh in range(MLA_HEADS):
                blk = slice(hh * HEAD_BLOCK, (hh + 1) * HEAD_BLOCK)
                qa = _dot(q_lat, wqa_ref[:, blk])
                qb = _dot(q_lat, wqb_ref[:, blk])
                q_s[hh, rows, :] = ((qa * cos + qb * sin) * scale).astype(jnp.bfloat16)
                k_s[hh, rows, :] = (_dot(kv_lat, wk_ref[:, blk]) + k_rope).astype(jnp.bfloat16)
            for p2 in range(MLA_HEADS // 2):
                blk = slice(p2 * HEAD_BLOCK, (p2 + 1) * HEAD_BLOCK)
                v_s[p2, rows, :] = _dot(kv_lat, wv_ref[:, blk]).astype(jnp.bfloat16)
            qc = z[:, o0 + 2 * HEAD_BLOCK:]
            mem_k = memkv_ref[0, :, :XATTN_WIDTH]
            mem_v = memkv_ref[0, :, XATTN_WIDTH:]
            cat_ref[0, rows, TOKEN_MIX_WIDTH:] = _xattn_rows(qc, mem_k, mem_v).astype(jnp.bfloat16)
            return carry

        lax.fori_loop(0, S // PROJ_ROWS, chunk, 0)

    first_half = lax.broadcasted_iota(jnp.int32, (1, HEAD_BLOCK), 1) < V_HEAD_DIM
    tile_lanes = ATTN_TILE // LANES

    def lane_fold(x, op):
        r = x[:, :LANES]
        for c in range(1, tile_lanes):
            r = op(r, x[:, c * LANES:(c + 1) * LANES])
        return r

    def q_block(i, carry0):
        rows = pl.ds(pl.multiple_of(i * ATTN_TILE, ATTN_TILE), ATTN_TILE)
        qs = (q_s[2 * hp, rows, :], q_s[2 * hp + 1, rows, :])

        def scores(c, mrun, masked):
            cols = pl.ds(pl.multiple_of(c * ATTN_TILE, ATTN_TILE), ATTN_TILE)
            out = []
            for t in range(2):
                s = _dot_nt(qs[t], k_s[2 * hp + t, cols, :])
                if masked:
                    q_idx = lax.broadcasted_iota(jnp.int32, s.shape, 0)
                    k_idx = lax.broadcasted_iota(jnp.int32, s.shape, 1)
                    s = jnp.where(k_idx <= q_idx, s, NEG_INF)
                sc_s[t, c] = s
                out.append(jnp.maximum(mrun[t], lane_fold(s, jnp.maximum)))
            return tuple(out)

        neg = jnp.full((ATTN_TILE, LANES), NEG_INF, jnp.float32)
        mrun = lax.fori_loop(0, i, functools.partial(scores, masked=False), (neg, neg))
        mrun = scores(i, mrun, True)
        ms = tuple(jnp.max(m, axis=-1, keepdims=True) for m in mrun)

        def weigh(c, carry):
            cols = pl.ds(pl.multiple_of(c * ATTN_TILE, ATTN_TILE), ATTN_TILE)
            v = v_s[hp, cols, :]
            out = []
            for t in range(2):
                lrun, acc = carry[t]
                p = jnp.exp2(sc_s[t, c] - ms[t])
                out.append((lrun + lane_fold(p, jnp.add), acc + _dot(p.astype(jnp.bfloat16), v)))
            return tuple(out)

        zero = jnp.zeros((ATTN_TILE, LANES), jnp.float32)
        (l0, acc0), (l1, acc1) = lax.fori_loop(0, i + 1, weigh, ((zero, zero), (zero, zero)))
        l0 = jnp.sum(l0, axis=-1, keepdims=True)
        l1 = jnp.sum(l1, axis=-1, keepdims=True)
        o_s[hp, rows, :] = jnp.where(first_half, acc0 / l0, acc1 / l1).astype(jnp.bfloat16)
        return carry0

    lax.fori_loop(0, S // ATTN_TILE, q_block, 0)

    @pl.when(hp == MLA_HEADS // 2 - 1)
    def _emit():
        for p2 in range(MLA_HEADS // 2):
            cat_ref[0, :, p2 * HEAD_BLOCK:(p2 + 1) * HEAD_BLOCK] = o_s[p2]


ROPE_POS_TILE = 2048


def _rope_kernel(pos_ref, invf_ref, cs_ref):
    half = QK_ROPE_DIM // 2
    ang = invf_ref[...] * pos_ref[...].astype(jnp.float32)
    cs_ref[:half, :] = jnp.cos(ang)
    cs_ref[half:, :] = jnp.sin(ang)


def _rope_table(positions):
    B, S = positions.shape
    T = B * S
    half = QK_ROPE_DIM // 2
    inv_freq = ROPE_THETA ** (-np.arange(0, QK_ROPE_DIM, 2, dtype=np.float32) / QK_ROPE_DIM)
    tile = min(ROPE_POS_TILE, T)
    cs = pl.pallas_call(
        _rope_kernel,
        out_shape=jax.ShapeDtypeStruct((QK_ROPE_DIM, T), jnp.float32),
        grid=(T // tile,),
        in_specs=[
            pl.BlockSpec((1, tile), lambda i: (0, i)),
            pl.BlockSpec((half, 1), lambda i: (0, 0)),
        ],
        out_specs=pl.BlockSpec((QK_ROPE_DIM, tile), lambda i: (0, i)),
        compiler_params=_cparams(("arbitrary",)),
        name="rope_table",
    )(positions.reshape(1, T), jnp.asarray(inv_freq.reshape(half, 1)))
    return cs.T.reshape(B, S, QK_ROPE_DIM)


def _mla_layer(z, rope_cs, memkv, g_q, g_kv, w_uq, w_ukv):
    B, S, _ = z.shape
    wq = w_uq.reshape(Q_LORA_RANK, MLA_HEADS, QK_HEAD_DIM)
    nope, x1, x2 = wq[..., :QK_NOPE_DIM], wq[..., QK_NOPE_DIM:QK_NOPE_DIM + 16], wq[..., QK_NOPE_DIM + 16:]
    zpad = jnp.zeros((Q_LORA_RANK, MLA_HEADS, HEAD_BLOCK - QK_HEAD_DIM), w_uq.dtype)
    wqa = jnp.concatenate([nope, x1, x2, zpad], -1).reshape(Q_LORA_RANK, MLA_HEADS * HEAD_BLOCK)
    wqb = jnp.concatenate([jnp.zeros_like(nope), -x2, x1, zpad], -1).reshape(Q_LORA_RANK, MLA_HEADS * HEAD_BLOCK)
    wkv = w_ukv.reshape(KV_LORA_RANK, MLA_HEADS, QK_NOPE_DIM + V_HEAD_DIM)
    wk = jnp.concatenate([wkv[..., :QK_NOPE_DIM], jnp.zeros_like(wkv[..., :QK_NOPE_DIM])], -1)
    wk = wk.reshape(KV_LORA_RANK, MLA_HEADS * HEAD_BLOCK)
    wv = wkv[..., QK_NOPE_DIM:].reshape(KV_LORA_RANK, MLA_HEADS * V_HEAD_DIM)
    half = QK_ROPE_DIM // 2
    ec = np.zeros((QK_ROPE_DIM, HEAD_BLOCK), np.float32)
    es = np.zeros((QK_ROPE_DIM, HEAD_BLOCK), np.float32)
    for f in range(half):
        ec[f, QK_NOPE_DIM + f] = ec[f, QK_NOPE_DIM + half + f] = 1.0
        es[half + f, QK_NOPE_DIM + f] = es[half + f, QK_NOPE_DIM + half + f] = 1.0
    bf = jnp.bfloat16
    const = lambda b, h: (0, 0)
    return pl.pallas_call(
        _mla_kernel,
        out_shape=jax.ShapeDtypeStruct((B, S, D_MODEL), jnp.bfloat16),
        grid=(B, MLA_HEADS // 2),
        in_specs=[
            pl.BlockSpec((1, S, Z_MLA), lambda b, h: (b, 0, 0)),
            pl.BlockSpec((1, S, QK_ROPE_DIM), lambda b, h: (b, 0, 0)),
            pl.BlockSpec((QK_ROPE_DIM, HEAD_BLOCK), const),
            pl.BlockSpec((QK_ROPE_DIM, HEAD_BLOCK), const),
            pl.BlockSpec((1, Q_LORA_RANK), const),
            pl.BlockSpec((1, KV_LORA_RANK), const),
            pl.BlockSpec((Q_LORA_RANK, MLA_HEADS * HEAD_BLOCK), const),
            pl.BlockSpec((Q_LORA_RANK, MLA_HEADS * HEAD_BLOCK), const),
            pl.BlockSpec((KV_LORA_RANK, MLA_HEADS * HEAD_BLOCK), const),
            pl.BlockSpec((KV_LORA_RANK, MLA_HEADS * V_HEAD_DIM), const),
            pl.BlockSpec((1, MEM_LEN, 2 * XATTN_WIDTH), lambda b, h: (b, 0, 0)),
        ],
        out_specs=pl.BlockSpec((1, S, D_MODEL), lambda b, h: (b, 0, 0)),
        scratch_shapes=[
            pltpu.VMEM((MLA_HEADS, S, HEAD_BLOCK), bf),
            pltpu.VMEM((MLA_HEADS, S, HEAD_BLOCK), bf),
            pltpu.VMEM((MLA_HEADS // 2, S, HEAD_BLOCK), bf),
            pltpu.VMEM((MLA_HEADS // 2, S, HEAD_BLOCK), bf),
            pltpu.VMEM((2, S // ATTN_TILE, ATTN_TILE, ATTN_TILE), jnp.float32),
        ],
        compiler_params=_cparams(("arbitrary", "arbitrary")),
        name="mla_mixer",
    )(z, rope_cs, jnp.asarray(ec, bf), jnp.asarray(es, bf), g_q.reshape(1, -1), g_kv.reshape(1, -1),
      wqa.astype(bf), wqb.astype(bf), wk.astype(bf), wv.astype(bf), memkv)


def _mla_in_weight(w_in):
    o0 = Q_LORA_RANK + KV_LORA_RANK
    x1, x2 = w_in[:, o0:o0 + 16], w_in[:, o0 + 16:o0 + 32]
    z64 = jnp.zeros((D_MODEL, QK_NOPE_DIM), w_in.dtype)
    z32 = jnp.zeros((D_MODEL, HEAD_BLOCK - QK_HEAD_DIM), w_in.dtype)
    return jnp.concatenate([w_in[:, :o0], z64, x1, x2, z32, z64, -x2, x1, z32, w_in[:, o0 + 32:]], axis=1)


def _shift_rows(x, k):
    rolled = pltpu.roll(x, k, 0)
    row = lax.broadcasted_iota(jnp.int32, x.shape, 0)
    return jnp.where(row >= k, rolled, 0.0)


def _pool_kernel(z_ref, wmix_ref, scale_ref, memkv_ref, cat_ref, d_s):
    S = z_ref.shape[1]
    step = (lax.broadcasted_iota(jnp.int32, (S, 1), 0) + 1).astype(jnp.float32)
    for ct in range(TOKEN_MIX_WIDTH // LANES):
        cols = slice(ct * LANES, (ct + 1) * LANES)
        u = z_ref[0, :, cols].astype(jnp.float32)
        groups = sorted({(ct * LANES) // POOL_GROUP_WIDTH, (ct * LANES + LANES - 1) // POOL_GROUP_WIDTH})
        sums, cur, w = {}, u, 1
        while w < POOL_WINDOWS[groups[-1]]:
            cur = cur + _shift_rows(cur, w)
            w *= 2
            sums[w] = cur
        mean = sums[POOL_WINDOWS[groups[0]]] / jnp.minimum(step, float(POOL_WINDOWS[groups[0]]))
        if len(groups) == 2:
            win = POOL_WINDOWS[groups[1]]
            col = ct * LANES + lax.broadcasted_iota(jnp.int32, (1, LANES), 1)
            mean = jnp.where(col < groups[1] * POOL_GROUP_WIDTH, mean, sums[win] / jnp.minimum(step, float(win)))
        d_s[:, cols] = (mean - u).astype(jnp.bfloat16)
    y = _dot(d_s[...], wmix_ref[...]) * scale_ref[...]
    cat_ref[0, :, :TOKEN_MIX_WIDTH] = y.astype(jnp.bfloat16)
    mem_k = memkv_ref[0, :, :XATTN_WIDTH]
    mem_v = memkv_ref[0, :, XATTN_WIDTH:]
    for xc in range(S // XATTN_ROWS):
        rows = pl.ds(xc * XATTN_ROWS, XATTN_ROWS)
        qc = z_ref[0, rows, TOKEN_MIX_WIDTH:]
        cat_ref[0, rows, TOKEN_MIX_WIDTH:] = _xattn_rows(qc, mem_k, mem_v).astype(jnp.bfloat16)


def _pool_layer(z, memkv, w_pool_mix, pool_scale):
    B, S, _ = z.shape
    wmix = jnp.zeros((TOKEN_MIX_WIDTH, TOKEN_MIX_WIDTH), w_pool_mix.dtype)
    for g in range(len(POOL_WINDOWS)):
        blk = slice(g * POOL_GROUP_WIDTH, (g + 1) * POOL_GROUP_WIDTH)
        wmix = wmix.at[blk, blk].set(w_pool_mix[g])
    return pl.pallas_call(
        _pool_kernel,
        out_shape=jax.ShapeDtypeStruct((B, S, D_MODEL), jnp.bfloat16),
        grid=(B,),
        in_specs=[
            pl.BlockSpec((1, S, D_MODEL), lambda b: (b, 0, 0)),
            pl.BlockSpec((TOKEN_MIX_WIDTH, TOKEN_MIX_WIDTH), lambda b: (0, 0)),
            pl.BlockSpec((1, TOKEN_MIX_WIDTH), lambda b: (0, 0)),
            pl.BlockSpec((1, MEM_LEN, 2 * XATTN_WIDTH), lambda b: (b, 0, 0)),
        ],
        out_specs=pl.BlockSpec((1, S, D_MODEL), lambda b: (b, 0, 0)),
        scratch_shapes=[pltpu.VMEM((S, TOKEN_MIX_WIDTH), jnp.bfloat16)],
        compiler_params=_cparams(("arbitrary",)),
        name="pool_mixer",
    )(z, wmix.astype(jnp.bfloat16), pool_scale.reshape(1, -1), memkv)


def _post_kernel(cat_ref, x_ref, wo_ref, g_ref, wr_hi_ref, wr_lo_ref, br_ref,
                 x1_ref, hp_ref, cls_ref, cnt_ref):
    i = pl.program_id(0)
    tm = x_ref.shape[0]
    x1 = x_ref[...] + _dot(cat_ref[...], wo_ref[...])
    x1_ref[...] = x1
    hn = _rms(x1, g_ref[...])
    hi = hn.astype(jnp.bfloat16)
    lo = (hn - hi.astype(jnp.float32)).astype(jnp.bfloat16)
    logits = (_dot(hi, wr_hi_ref[...]) + _dot(lo, wr_hi_ref[...]) + _dot(hi, wr_lo_ref[...])
              + br_ref[...])
    lane = lax.broadcasted_iota(jnp.int32, (tm, LANES), 1)
    lane_f = lane.astype(jnp.float32)

    def first_argmax(v):
        m = jnp.max(v, axis=-1, keepdims=True)
        idx = jnp.min(jnp.where(v == m, lane_f, float(LANES)), axis=-1, keepdims=True)
        return m, idx.astype(jnp.int32)

    is_g = lane < N_EXPERT_GROUPS
    gmax, g_sel = first_argmax(jnp.where(is_g, logits, NEG_INF))
    gate_g = 1.0 / jnp.sum(jnp.where(is_g, jnp.exp(logits - gmax), 0.0), axis=-1, keepdims=True)
    lo_lane = N_EXPERT_GROUPS + EXPERTS_PER_GROUP * g_sel
    le = jnp.where((lane >= lo_lane) & (lane < lo_lane + EXPERTS_PER_GROUP), logits, NEG_INF)
    m1, i1 = first_argmax(le)
    m2, i2 = first_argmax(jnp.where(lane == i1, NEG_INF, le))
    e2 = jnp.exp(m2 - m1)
    w1 = gate_g / (1.0 + e2)
    w2 = gate_g * e2 / (1.0 + e2)
    j1, j2 = i1 - lo_lane, i2 - lo_lane
    first_low = j1 < j2
    a = jnp.where(first_low, j1, j2)
    b = jnp.where(first_low, j2, j1)
    w_a = jnp.where(first_low, w1, w2)
    w_b = jnp.where(first_low, w2, w1)
    cls = g_sel * PAIRS_PER_GROUP + ((a * (15 - a)) >> 1) + (b - a - 1)

    hp_ref[:, :HALF] = _pack_halves(hn)
    wa_bits = _bits(jnp.broadcast_to(w_a, (tm, LANES)))
    wb_bits = _bits(jnp.broadcast_to(w_b, (tm, LANES)))
    hp_ref[:, HALF:] = jnp.where(lane == 0, wa_bits, jnp.where(lane == 1, wb_bits, jnp.uint32(0)))

    cls_l0 = jnp.where(lane == 0, cls.astype(jnp.float32), 0.0).astype(jnp.bfloat16)
    sel = (lax.broadcasted_iota(jnp.int32, (8, LANES), 1) == 0).astype(jnp.bfloat16)
    cls_row = _dot_nt(sel, cls_l0)
    cls_ref[0] = cls_row.astype(jnp.int32)

    cid = lax.broadcasted_iota(jnp.int32, (CLASS_PAD, tm), 0).astype(jnp.float32)
    onehot_t = (cid == jnp.broadcast_to(cls_row[0:1, :], (CLASS_PAD, tm))).astype(jnp.float32)
    tile_cnt = jnp.broadcast_to(jnp.sum(onehot_t, axis=-1, keepdims=True), (CLASS_PAD, LANES))

    @pl.when(i == 0)
    def _init():
        cnt_ref[...] = tile_cnt

    @pl.when(i > 0)
    def _acc():
        cnt_ref[...] = cnt_ref[...] + tile_cnt


def _post(cat2d, x2d, w_o, g_ffn, w_rg, b_rg, w_re, b_re):
    T, D = x2d.shape
    n = T // TOKEN_TILE
    wr = jnp.zeros((D, LANES), jnp.float32)
    wr = wr.at[:, :N_EXPERT_GROUPS].set(w_rg).at[:, N_EXPERT_GROUPS:N_EXPERT_GROUPS + N_EXPERTS].set(w_re)
    wr_hi = wr.astype(jnp.bfloat16)
    wr_lo = (wr - wr_hi.astype(jnp.float32)).astype(jnp.bfloat16)
    br = jnp.zeros((1, LANES), jnp.float32)
    br = br.at[0, :N_EXPERT_GROUPS].set(b_rg).at[0, N_EXPERT_GROUPS:N_EXPERT_GROUPS + N_EXPERTS].set(b_re)
    const = lambda i: (0, 0)
    return pl.pallas_call(
        _post_kernel,
        out_shape=(
            jax.ShapeDtypeStruct((T, D), jnp.float32),
            jax.ShapeDtypeStruct((T, PACKED_ROW), jnp.uint32),
            jax.ShapeDtypeStruct((n, 8, TOKEN_TILE), jnp.int32),
            jax.ShapeDtypeStruct((CLASS_PAD, LANES), jnp.float32),
        ),
        grid=(n,),
        in_specs=[
            pl.BlockSpec((TOKEN_TILE, D), lambda i: (i, 0)),
            pl.BlockSpec((TOKEN_TILE, D), lambda i: (i, 0)),
            pl.BlockSpec((D, D), const),
            pl.BlockSpec((1, D), const),
            pl.BlockSpec((D, LANES), const),
            pl.BlockSpec((D, LANES), const),
            pl.BlockSpec((1, LANES), const),
        ],
        out_specs=(
            pl.BlockSpec((TOKEN_TILE, D), lambda i: (i, 0)),
            pl.BlockSpec((TOKEN_TILE, PACKED_ROW), lambda i: (i, 0)),
            pl.BlockSpec((1, 8, TOKEN_TILE), lambda i: (i, 0, 0)),
            pl.BlockSpec((CLASS_PAD, LANES), const),
        ),
        compiler_params=_cparams(("arbitrary",)),
        name="out_proj_router",
    )(cat2d, x2d, w_o.astype(jnp.bfloat16), g_ffn.reshape(1, D), wr_hi, wr_lo, br)


def _max_expert_tiles(T):
    return T // EXPERT_TILE + N_CLASSES


def _slots_kernel(cls_ref, cnt_ref, tri_ref, ltri_ref, pos_ref, tcls_ref, base_ref):
    i = pl.program_id(0)
    tm = cls_ref.shape[2]
    n_lane = tcls_ref.shape[1]

    @pl.when(i == 0)
    def _offsets():
        tiles = jnp.floor((cnt_ref[...] + (EXPERT_TILE - 1)) * (1.0 / EXPERT_TILE))
        before = _dot(ltri_ref[...], tiles.astype(jnp.bfloat16))
        base_ref[...] = before * EXPERT_TILE
        upto = (before + tiles)[:, 0:1]
        k = lax.broadcasted_iota(jnp.int32, (CLASS_PAD, n_lane), 1).astype(jnp.float32)
        cid = lax.broadcasted_iota(jnp.int32, (CLASS_PAD, n_lane), 0)
        done = jnp.where((upto <= k) & (cid < N_CLASSES), 1.0, 0.0)
        tile_cls = jnp.sum(done, axis=0, keepdims=True)
        total = jnp.max(jnp.where(cid < N_CLASSES, jnp.broadcast_to(upto, (CLASS_PAD, n_lane)), 0.0),
                        axis=0, keepdims=True)
        row = lax.broadcasted_iota(jnp.int32, (8, n_lane), 0)
        tcls_ref[...] = jnp.where(row == 0, jnp.broadcast_to(tile_cls, (8, n_lane)),
                                  jnp.broadcast_to(total, (8, n_lane))).astype(jnp.int32)

    cls_row = cls_ref[0, 0:1, :]
    cid = lax.broadcasted_iota(jnp.int32, (CLASS_PAD, tm), 0)
    onehot_t = cid == jnp.broadcast_to(cls_row, (CLASS_PAD, tm))
    oh = onehot_t.astype(jnp.bfloat16)
    earlier = _dot(oh, tri_ref[...])
    base = base_ref[:, 0:1]
    slot = jnp.sum(jnp.where(onehot_t, earlier + base, 0.0), axis=0, keepdims=True)
    pos_ref[0] = slot.astype(jnp.int32)
    base_ref[...] = base_ref[...] + jnp.sum(onehot_t.astype(jnp.float32), axis=-1, keepdims=True)


def _slots(cls, cnt):
    n, _, tm = cls.shape
    T = n * tm
    n_lane = pl.cdiv(_max_expert_tiles(T), LANES) * LANES
    tri = jnp.asarray(np.triu(np.ones((tm, tm), np.float32), 1), jnp.bfloat16)
    ltri = jnp.asarray(np.tril(np.ones((CLASS_PAD, CLASS_PAD), np.float32), -1), jnp.bfloat16)
    return pl.pallas_call(
        _slots_kernel,
        out_shape=(
            jax.ShapeDtypeStruct((n, 1, tm), jnp.int32),
            jax.ShapeDtypeStruct((8, n_lane), jnp.int32),
        ),
        grid=(n,),
        in_specs=[
            pl.BlockSpec((1, 8, tm), lambda i: (i, 0, 0)),
            pl.BlockSpec((CLASS_PAD, LANES), lambda i: (0, 0)),
            pl.BlockSpec((tm, tm), lambda i: (0, 0)),
            pl.BlockSpec((CLASS_PAD, CLASS_PAD), lambda i: (0, 0)),
        ],
        out_specs=(
            pl.BlockSpec((1, 1, tm), lambda i: (i, 0, 0)),
            pl.BlockSpec((8, n_lane), lambda i: (0, 0)),
        ),
        scratch_shapes=[pltpu.VMEM((CLASS_PAD, LANES), jnp.float32)],
        compiler_params=_cparams(("arbitrary",)),
        name="moe_slots",
    )(cls, cnt, tri, ltri)


def _row_copy(src_ref, src_row, dst_ref, dst_row, sem):
    return pltpu.make_async_copy(src_ref.at[pl.ds(src_row, 1), :], dst_ref.at[pl.ds(dst_row, 1), :], sem)


def _dispatch_kernel(pos_ref, hp_ref, xs_in_ref, xs_ref, sem):
    del xs_in_ref
    tm = hp_ref.shape[0]

    def start(r, c):
        _row_copy(hp_ref, r, xs_ref, pos_ref[0, 0, r], sem).start()
        return c

    lax.fori_loop(0, tm, start, 0, unroll=8)

    def wait(r, c):
        _row_copy(hp_ref, 0, xs_ref, 0, sem).wait()
        return c

    lax.fori_loop(0, tm, wait, 0, unroll=8)


def _dispatch(pos, hp, n_rows):
    T, W = hp.shape
    n = T // TOKEN_TILE
    return pl.pallas_call(
        _dispatch_kernel,
        out_shape=jax.ShapeDtypeStruct((n_rows, W), jnp.uint32),
        grid=(n,),
        in_specs=[
            pl.BlockSpec((1, 1, TOKEN_TILE), lambda i: (i, 0, 0), memory_space=pltpu.SMEM),
            pl.BlockSpec((TOKEN_TILE, W), lambda i: (i, 0)),
            pl.BlockSpec(memory_space=pl.ANY),
        ],
        out_specs=pl.BlockSpec(memory_space=pl.ANY),
        scratch_shapes=[pltpu.SemaphoreType.DMA],
        input_output_aliases={2: 0},
        compiler_params=_cparams(("arbitrary",)),
        name="moe_dispatch",
    )(pos, hp, jnp.zeros((n_rows, W), jnp.uint32))


def _expert_kernel(grp_ref, ta_ref, tb_ref, nt_ref, xs_ref, wg_hbm, wu_hbm, wd_hbm, ys_ref,
                   wgu_s, wd_s, stg_g, stg_u, stg_d, sems):
    i = pl.program_id(0)
    prev = jnp.maximum(i - 1, 0)
    live = i < nt_ref[0]

    @pl.when(live & ((i == 0) | (grp_ref[i] != grp_ref[prev])))
    def _load_group():
        e0 = grp_ref[i] * EXPERTS_PER_GROUP

        def copies(e, slot):
            return (pltpu.make_async_copy(wg_hbm.at[e0 + e], stg_g.at[slot], sems.at[slot]),
                    pltpu.make_async_copy(wu_hbm.at[e0 + e], stg_u.at[slot], sems.at[slot]),
                    pltpu.make_async_copy(wd_hbm.at[e0 + e], stg_d.at[slot], sems.at[slot]))

        for c in copies(0, 0):
            c.start()
        for e in range(EXPERTS_PER_GROUP):
            slot = e % 2
            if e + 1 < EXPERTS_PER_GROUP:
                for c in copies(e + 1, 1 - slot):
                    c.start()
            for c in copies(e, slot):
                c.wait()
            wgu_s[e, :, :D_EXPERT] = stg_g[slot].astype(jnp.bfloat16)
            wgu_s[e, :, D_EXPERT:] = stg_u[slot].astype(jnp.bfloat16)
            wd_s[e] = stg_d[slot].astype(jnp.bfloat16)

    @pl.when(live)
    def _run():
        lo, hi = _unpack_halves(xs_ref[:, :HALF])
        lo = lo.astype(jnp.bfloat16)
        hi = hi.astype(jnp.bfloat16)
        gates = lax.bitcast_convert_type(xs_ref[:, HALF:], jnp.float32)
        y = None
        for e, lane in ((ta_ref[i], 0), (tb_ref[i], 1)):
            gu = _dot(lo, wgu_s[e, :HALF, :]) + _dot(hi, wgu_s[e, HALF:, :])
            g, u = gu[:, :D_EXPERT], gu[:, D_EXPERT:]
            act = (g * jax.nn.sigmoid(g)) * u * gates[:, lane:lane + 1]
            part = _dot(act.astype(jnp.bfloat16), wd_s[e])
            y = part if y is None else y + part
        ys_ref[...] = _pack_halves(y)

    @pl.when(jnp.logical_not(live))
    def _unused():
        ys_ref[...] = jnp.zeros(ys_ref.shape, ys_ref.dtype)


def _experts(xs, tile_grp, tile_a, tile_b, n_tiles, w_gate, w_up, w_down):
    n_rows = xs.shape[0]
    max_tiles = n_rows // EXPERT_TILE
    bf = jnp.bfloat16

    def row_map(i, grp, ta, tb, nt):
        return (jnp.minimum(i, nt[0] - 1), 0)

    hbm = pl.BlockSpec(memory_space=pl.ANY)
    return pl.pallas_call(
        _expert_kernel,
        out_shape=jax.ShapeDtypeStruct((n_rows, HALF), jnp.uint32),
        grid_spec=pltpu.PrefetchScalarGridSpec(
            num_scalar_prefetch=4,
            grid=(max_tiles,),
            in_specs=[pl.BlockSpec((EXPERT_TILE, PACKED_ROW), row_map), hbm, hbm, hbm],
            out_specs=pl.BlockSpec((EXPERT_TILE, HALF), lambda i, grp, ta, tb, nt: (i, 0)),
            scratch_shapes=[
                pltpu.VMEM((EXPERTS_PER_GROUP, D_MODEL, 2 * D_EXPERT), bf),
                pltpu.VMEM((EXPERTS_PER_GROUP, D_EXPERT, D_MODEL), bf),
                pltpu.VMEM((2, D_MODEL, D_EXPERT), jnp.float32),
                pltpu.VMEM((2, D_MODEL, D_EXPERT), jnp.float32),
                pltpu.VMEM((2, D_EXPERT, D_MODEL), jnp.float32),
                pltpu.SemaphoreType.DMA((2,)),
            ],
        ),
        compiler_params=_cparams(("arbitrary",)),
        name="moe_experts",
    )(tile_grp, tile_a, tile_b, n_tiles, xs, w_gate, w_up, w_down)


def _gathered_rows(pos_ref, pos_next_ref, ys_ref, ybuf, sems):
    i = pl.program_id(0)
    n = pl.num_programs(0)
    tm = ybuf.shape[1]
    slot = i % 2

    def start_tile(p_ref, dst_slot):
        def start(r, c):
            _row_copy(ys_ref, p_ref[0, 0, r], ybuf.at[dst_slot], r, sems.at[dst_slot]).start()
            return c

        lax.fori_loop(0, tm, start, 0, unroll=8)

    @pl.when(i == 0)
    def _prime():
        start_tile(pos_ref, 0)

    @pl.when(i + 1 < n)
    def _ahead():
        start_tile(pos_next_ref, 1 - slot)

    def wait(r, c):
        _row_copy(ys_ref, 0, ybuf.at[slot], 0, sems.at[slot]).wait()
        return c

    lax.fori_loop(0, tm, wait, 0, unroll=8)
    lo, hi = _unpack_halves(ybuf[slot])
    return jnp.concatenate([lo, hi], axis=1)


def _combine_proj_kernel(pos_ref, pos_next_ref, x1_ref, ys_ref, g_ref, w_ref, x2_ref, z_ref, ybuf, sems):
    x2 = x1_ref[...] + _gathered_rows(pos_ref, pos_next_ref, ys_ref, ybuf, sems)
    x2_ref[...] = x2
    h = _rms(x2, g_ref[...]).astype(jnp.bfloat16)
    z_ref[...] = _dot(h, w_ref[...]).astype(jnp.bfloat16)


def _combine_final_kernel(pos_ref, pos_next_ref, x1_ref, ys_ref, g_ref, out_ref, ybuf, sems):
    out_ref[...] = _rms(x1_ref[...] + _gathered_rows(pos_ref, pos_next_ref, ys_ref, ybuf, sems), g_ref[...])


def _combine(pos, x1, ys, g, w_next=None):
    T, D = x1.shape
    n = T // TOKEN_TILE
    in_specs = [
        pl.BlockSpec((1, 1, TOKEN_TILE), lambda i: (i, 0, 0), memory_space=pltpu.SMEM),
        pl.BlockSpec((1, 1, TOKEN_TILE), lambda i: (jnp.minimum(i + 1, n - 1), 0, 0), memory_space=pltpu.SMEM),
        pl.BlockSpec((TOKEN_TILE, D), lambda i: (i, 0)),
        pl.BlockSpec(memory_space=pl.ANY),
        pl.BlockSpec((1, D), lambda i: (0, 0)),
    ]
    scratch = [pltpu.VMEM((2, TOKEN_TILE, HALF), jnp.uint32), pltpu.SemaphoreType.DMA((2,))]
    row_spec = pl.BlockSpec((TOKEN_TILE, D), lambda i: (i, 0))
    if w_next is None:
        return pl.pallas_call(
            _combine_final_kernel,
            out_shape=jax.ShapeDtypeStruct((T, D), jnp.float32),
            grid=(n,), in_specs=in_specs, out_specs=row_spec, scratch_shapes=scratch,
            compiler_params=_cparams(("arbitrary",)), name="moe_combine_final",
        )(pos, pos, x1, ys, g.reshape(1, D))
    N = w_next.shape[1]
    return pl.pallas_call(
        _combine_proj_kernel,
        out_shape=(jax.ShapeDtypeStruct((T, D), jnp.float32), jax.ShapeDtypeStruct((T, N), jnp.bfloat16)),
        grid=(n,),
        in_specs=in_specs + [pl.BlockSpec((D, N), lambda i: (0, 0))],
        out_specs=(row_spec, pl.BlockSpec((TOKEN_TILE, N), lambda i: (i, 0))),
        scratch_shapes=scratch,
        compiler_params=_cparams(("arbitrary",)), name="moe_combine_proj",
    )(pos, pos, x1, ys, g.reshape(1, D), w_next)


def _moe_sorted(cat2d, x2d, w_o, g_ffn, w_rg, b_rg, w_re, b_re, w_gate, w_up, w_down):
    T = x2d.shape[0]
    x1, hp, cls, cnt = _post(cat2d, x2d, w_o, g_ffn, w_rg, b_rg, w_re, b_re)
    pos, tcls = _slots(cls, cnt)
    max_tiles = _max_expert_tiles(T)
    n_tiles = tcls[1, 0:1]
    tile_cls = jnp.minimum(tcls[0, :max_tiles], tcls[0, jnp.maximum(n_tiles[0] - 1, 0)])
    tile_grp = tile_cls // PAIRS_PER_GROUP
    tile_a = jnp.asarray(_CLASS_EA % EXPERTS_PER_GROUP)[tile_cls]
    tile_b = jnp.asarray(_CLASS_EB % EXPERTS_PER_GROUP)[tile_cls]
    xs = _dispatch(pos, hp, max_tiles * EXPERT_TILE)
    ys = _experts(xs, tile_grp, tile_a, tile_b, n_tiles, w_gate, w_up, w_down)
    return x1, pos, ys


def kernel(x, mem, positions, g_mix, w_in_mla, g_q_lat, w_uq, g_kv_lat, w_ukv, w_in_pool, w_pool_mix, pool_scale, g_mem, w_mem_kv, w_o, g_ffn, w_router_group, b_router_group, w_router_expert, b_router_expert, w_expert_gate, w_expert_up, w_expert_down, g_final):
    B, S, D = x.shape
    T = B * S
    depth = g_mix.shape[0]
    bf = jnp.bfloat16
    memkv = _memkv(mem, g_mem, w_mem_kv)
    rope_cs = _rope_table(positions)
    x2d = x.reshape(T, D)
    z = _in_proj(x2d, g_mix[0], _mla_in_weight(w_in_mla[0]).astype(bf))
    for i in range(depth):
        j = i // 2
        if i % 2 == 0:
            cat = _mla_layer(z.reshape(B, S, -1), rope_cs, memkv, g_q_lat[j], g_kv_lat[j], w_uq[j], w_ukv[j])
        else:
            cat = _pool_layer(z.reshape(B, S, -1), memkv, w_pool_mix[j], pool_scale[j])
        x1, pos, ys = _moe_sorted(
            cat.reshape(T, D), x2d, w_o[i], g_ffn[i], w_router_group[i], b_router_group[i],
            w_router_expert[i], b_router_expert[i], w_expert_gate[i], w_expert_up[i], w_expert_down[i])
        if i + 1 == depth:
            return _combine(pos, x1, ys, g_final).reshape(B, S, D)
        nxt = i + 1
        w_next = (_mla_in_weight(w_in_mla[nxt // 2]) if nxt % 2 == 0 else w_in_pool[nxt // 2]).astype(bf)
        x2d, z = _combine(pos, x1, ys, g_mix[nxt], w_next)
```
